```python
import jax, jax.numpy as jnp
from jax import lax
import numpy as np

D_MODEL = 1024
BATCH = 8
SEQ = 2048
DEPTH = 2
DEC_BATCH = 16
DEC_SEQ = 4096
PAST_LEN = 128

GRID_W = 64
HEAD_DIM = 64
Q_BLOCK = 128
EPS = 1e-6
ROPE_THETA = 10000.0
A_HEADS = 6
A_KV_HEADS = 2
B_HEADS = 5
NA_ROWS = 8
NA_COLS = 16
C_HEADS = 5
C_NOPE = 64
C_ROPE = 32
C_V = 64
C_Q_RANK = 256
C_KV_RANK = 128
A_Q = A_HEADS * HEAD_DIM
A_KV = A_KV_HEADS * HEAD_DIM
B_W = B_HEADS * HEAD_DIM
MIX_WIDTH = A_HEADS * HEAD_DIM + B_HEADS * HEAD_DIM + C_HEADS * C_V
IN_SIZES = (A_Q, A_KV, A_KV, B_W, B_W, B_W, C_Q_RANK, C_KV_RANK, C_ROPE)
IN_WIDTH = A_Q + 2 * A_KV + 3 * B_W + C_Q_RANK + C_KV_RANK + C_ROPE
PEER_HEADS = 8
N_KEYS = 128
N_EXPERTS = N_KEYS * N_KEYS
PEER_TOPK = 16
PEER_DKEY = 128
PEER_CHUNK = 128

kernel_name = "hybrid_parallel_group_encoder"


def rmsnorm(x, g):
    x32 = x.astype(jnp.float32)
    y = x32 * lax.rsqrt(jnp.mean(x32 * x32, axis=-1, keepdims=True) + EPS)
    return (y * g.astype(jnp.float32)).astype(x.dtype)


def axial_rope_tables(S, dim):
    dq = dim // 4
    freqs = ROPE_THETA ** (-jnp.arange(dq, dtype=jnp.float32) / dq)
    t = jnp.arange(S, dtype=jnp.int32)
    row = (t // GRID_W).astype(jnp.float32)
    col = (t % GRID_W).astype(jnp.float32)
    ang = jnp.stack([row[:, None] * freqs, col[:, None] * freqs], axis=1)
    return jnp.cos(ang), jnp.sin(ang)


def apply_axial_rope(x, cos, sin):
    B, S, H, dim = x.shape
    xr = x.reshape(B, S, H, 2, 2, dim // 4)
    x1, x2 = xr[..., 0, :], xr[..., 1, :]
    c = cos[None, :, None].astype(x.dtype)
    s = sin[None, :, None].astype(x.dtype)
    out = jnp.stack([x1 * c - x2 * s, x1 * s + x2 * c], axis=-2)
    return out.reshape(B, S, H, dim)


def block_attention(q, k, v):
    B, S, Hq, Dq = q.shape
    Hk, Dv = k.shape[2], v.shape[3]
    G = Hq // Hk
    scale = Dq ** -0.5
    qb = q.reshape(B, S // Q_BLOCK, Q_BLOCK, Hk, G, Dq).transpose(1, 0, 2, 3, 4, 5)

    def blk(qi):
        s = jnp.einsum('bqhgd,bkhd->bhgqk', qi, k) * scale
        p = jax.nn.softmax(s.astype(jnp.float32), axis=-1).astype(v.dtype)
        return jnp.einsum('bhgqk,bkhd->bqhgd', p, v)

    o = lax.map(blk, qb)
    return o.transpose(1, 0, 2, 3, 4, 5).reshape(B, S, Hq, Dv)


def neighborhood_attention(q, k, v, rpb):
    B, S, H, D = q.shape
    rows = S // GRID_W
    wr = min(NA_ROWS, rows)
    scale = D ** -0.5
    cols = np.arange(GRID_W)
    cs = np.clip(cols - NA_COLS // 2, 0, GRID_W - NA_COLS)
    col_idx = cs[:, None] + np.arange(NA_COLS)[None, :]
    col_bias_idx = col_idx - cols[:, None] + (NA_COLS - 1)
    qg = q.reshape(B, rows, GRID_W, H, D)
    kg = k.reshape(B, rows, GRID_W, H, D)
    vg = v.reshape(B, rows, GRID_W, H, D)

    def row_fn(r):
        rs = jnp.clip(r - wr // 2, 0, rows - wr)
        kb = lax.dynamic_slice_in_dim(kg, rs, wr, axis=1)[:, :, col_idx]
        vb = lax.dynamic_slice_in_dim(vg, rs, wr, axis=1)[:, :, col_idx]
        qr = lax.dynamic_index_in_dim(qg, r, axis=1, keepdims=False)
        row_bias_idx = rs + jnp.arange(wr, dtype=jnp.int32) - r + (NA_ROWS - 1)
        bias = jnp.take(rpb, row_bias_idx, axis=1)[:, :, col_bias_idx]
        s = jnp.einsum('bqhd,biqjhd->bhqij', qr, kb) * scale
        s = s.astype(jnp.float32) + bias.transpose(0, 2, 1, 3)[None].astype(jnp.float32)
        p = jax.nn.softmax(s.reshape(B, H, GRID_W, wr * NA_COLS), axis=-1)
        p = p.reshape(s.shape).astype(v.dtype)
        return jnp.einsum('bhqij,biqjhd->bqhd', p, vb)

    o = lax.map(row_fn, jnp.arange(rows, dtype=jnp.int32))
    return o.transpose(1, 0, 2, 3, 4).reshape(B, S, H, D)


def mixer(h, w_in, a_qn, a_kn, b_rpb, c_qn, c_wqb, c_kvn, c_wkvb, w_out, rope_a, rope_c):
    B, S, _ = h.shape
    z = h @ w_in
    splits = []
    acc = 0
    for n in IN_SIZES[:-1]:
        acc += n
        splits.append(acc)
    aq, ak, av, bq, bk, bv, cqa, ckv, ckpe = jnp.split(z, splits, axis=-1)

    aq = apply_axial_rope(rmsnorm(aq.reshape(B, S, A_HEADS, HEAD_DIM), a_qn), *rope_a)
    ak = apply_axial_rope(rmsnorm(ak.reshape(B, S, A_KV_HEADS, HEAD_DIM), a_kn), *rope_a)
    av = av.reshape(B, S, A_KV_HEADS, HEAD_DIM)
    o_a = block_attention(aq, ak, av).reshape(B, S, A_HEADS * HEAD_DIM)

    o_b = neighborhood_attention(bq.reshape(B, S, B_HEADS, HEAD_DIM),
                                 bk.reshape(B, S, B_HEADS, HEAD_DIM),
                                 bv.reshape(B, S, B_HEADS, HEAD_DIM), b_rpb)
    o_b = o_b.reshape(B, S, B_HEADS * HEAD_DIM)

    cq = (rmsnorm(cqa, c_qn) @ c_wqb).reshape(B, S, C_HEADS, C_NOPE + C_ROPE)
    q_nope, q_pe = cq[..., :C_NOPE], cq[..., C_NOPE:]
    q_pe = apply_axial_rope(q_pe, *rope_c)
    kv = (rmsnorm(ckv, c_kvn) @ c_wkvb).reshape(B, S, C_HEADS, C_NOPE + C_V)
    k_nope, v_c = kv[..., :C_NOPE], kv[..., C_NOPE:]
    k_pe = apply_axial_rope(ckpe[:, :, None, :], *rope_c)
    q_c = jnp.concatenate([q_nope, q_pe], axis=-1)
    k_c = jnp.concatenate([k_nope, jnp.broadcast_to(k_pe, (B, S, C_HEADS, C_ROPE))], axis=-1)
    o_c = block_attention(q_c, k_c, v_c).reshape(B, S, C_HEADS * C_V)

    return jnp.concatenate([o_a, o_b, o_c], axis=-1) @ w_out


def peer(h, wq, subkeys, u, v):
    B, S, D = h.shape
    T = B * S
    hc = h.reshape(T // PEER_CHUNK, PEER_CHUNK, D)

    def chunk_fn(xc):
        q = (xc @ wq).reshape(PEER_CHUNK, PEER_HEADS, 2, PEER_DKEY // 2)
        s = jnp.einsum('thpd,hpnd->thpn', q, subkeys)
        sv, si = lax.top_k(s, PEER_TOPK)
        cand = (sv[..., 0, :, None] + sv[..., 1, None, :]).reshape(PEER_CHUNK, PEER_HEADS, PEER_TOPK * PEER_TOPK)
        cidx = (si[..., 0, :, None] * N_KEYS + si[..., 1, None, :]).reshape(PEER_CHUNK, PEER_HEADS, PEER_TOPK * PEER_TOPK)
        tv, ti = lax.top_k(cand, PEER_TOPK)
        eidx = jnp.take_along_axis(cidx, ti, axis=-1)
        g = jax.nn.softmax(tv.astype(jnp.float32), axis=-1).astype(xc.dtype)
        us = jnp.take(u, eidx, axis=0)
        vs = jnp.take(v, eidx, axis=0)
        a = jax.nn.gelu(jnp.einsum('thkd,td->thk', us, xc), approximate=False)
        return jnp.einsum('thk,thkd->td', g * a, vs)

    return lax.map(chunk_fn, hc).reshape(B, S, D)


def trunk(x, c, ada_w, ada_b, norm1_g, norm2_g, w_in, a_q_norm, a_k_norm, b_rpb,
          c_q_norm, c_wqb, c_kv_norm, c_wkvb, w_out, peer_wq, peer_subkeys, peer_u, peer_v, final_g):
    S = x.shape[1]
    rope_a = axial_rope_tables(S, HEAD_DIM)
    rope_c = axial_rope_tables(S, C_ROPE)
    for l in range(DEPTH):
        mod = (jax.nn.silu(c) @ ada_w[l] + ada_b[l])[:, None, :]
        sh1, sc1, g1, sh2, sc2, g2 = jnp.split(mod, 6, axis=-1)
        h = rmsnorm(x, norm1_g[l]) * (1 + sc1) + sh1
        x = x + g1 * mixer(h, w_in[l], a_q_norm[l], a_k_norm[l], b_rpb[l], c_q_norm[l], c_wqb[l],
                           c_kv_norm[l], c_wkvb[l], w_out[l], rope_a, rope_c)
        h = rmsnorm(x, norm2_g[l]) * (1 + sc2) + sh2
        x = x + g2 * peer(h, peer_wq[l], peer_subkeys[l], peer_u[l], peer_v[l])
    return rmsnorm(x, final_g)


def setup_inputs(seed: int = 0) -> dict:
    key = jax.random.key(seed)
    ks = jax.random.split(key, 24)

    def nrm(k, shape, scale):
        return jax.random.normal(k, shape, jnp.float32) * scale

    D = D_MODEL
    return {
        'x_prompt': nrm(ks[0], (BATCH, SEQ, D), 1.0),
        'x_sample': nrm(ks[1], (DEC_BATCH, DEC_SEQ, D), 1.0),
        'c_prompt': nrm(ks[2], (BATCH, D), 1.0),
        'c_sample': nrm(ks[3], (DEC_BATCH, D), 1.0),
        'ada_w': nrm(ks[4], (DEPTH, D, 6 * D), D ** -0.5),
        'ada_b': nrm(ks[5], (DEPTH, 6 * D), 0.02),
        'norm1_g': 1.0 + nrm(ks[6], (DEPTH, D), 0.02),
        'norm2_g': 1.0 + nrm(ks[7], (DEPTH, D), 0.02),
        'w_in': nrm(ks[8], (DEPTH, D, IN_WIDTH), D ** -0.5),
        'a_q_norm': 1.0 + nrm(ks[9], (DEPTH, HEAD_DIM), 0.02),
        'a_k_norm': 1.0 + nrm(ks[10], (DEPTH, HEAD_DIM), 0.02),
        'b_rpb': nrm(ks[11], (DEPTH, B_HEADS, 2 * NA_ROWS - 1, 2 * NA_COLS - 1), 0.1),
        'c_q_norm': 1.0 + nrm(ks[12], (DEPTH, C_Q_RANK), 0.02),
        'c_wqb': nrm(ks[13], (DEPTH, C_Q_RANK, C_HEADS * (C_NOPE + C_ROPE)), C_Q_RANK ** -0.5),
        'c_kv_norm': 1.0 + nrm(ks[14], (DEPTH, C_KV_RANK), 0.02),
        'c_wkvb': nrm(ks[15], (DEPTH, C_KV_RANK, C_HEADS * (C_NOPE + C_V)), C_KV_RANK ** -0.5),
        'w_out': nrm(ks[16], (DEPTH, MIX_WIDTH, D), MIX_WIDTH ** -0.5),
        'peer_wq': nrm(ks[17], (DEPTH, D, PEER_HEADS * PEER_DKEY), D ** -0.5),
        'peer_subkeys': nrm(ks[18], (DEPTH, PEER_HEADS, 2, N_KEYS, PEER_DKEY // 2), (PEER_DKEY // 2) ** -0.5),
        'peer_u': nrm(ks[19], (DEPTH, N_EXPERTS, D), D ** -0.5),
        'peer_v': nrm(ks[20], (DEPTH, N_EXPERTS, D), PEER_HEADS ** -0.5),
        'final_g': 1.0 + nrm(ks[21], (D,), 0.02),
    }


def reference(x_prompt, x_sample, c_prompt, c_sample, ada_w, ada_b, norm1_g, norm2_g, w_in,
              a_q_norm, a_k_norm, b_rpb, c_q_norm, c_wqb, c_kv_norm, c_wkvb, w_out,
              peer_wq, peer_subkeys, peer_u, peer_v, final_g):
    y_prompt = trunk(x_prompt, c_prompt, ada_w, ada_b, norm1_g, norm2_g, w_in, a_q_norm, a_k_norm, b_rpb,
                     c_q_norm, c_wqb, c_kv_norm, c_wkvb, w_out, peer_wq, peer_subkeys, peer_u, peer_v, final_g)
    y_sample = trunk(x_sample, c_sample, ada_w, ada_b, norm1_g, norm2_g, w_in, a_q_norm, a_k_norm, b_rpb,
                     c_q_norm, c_wqb, c_kv_norm, c_wkvb, w_out, peer_wq, peer_subkeys, peer_u, peer_v, final_g)
    return (y_prompt, y_sample)
```

```python
import functools

import numpy as np
import jax
import jax.numpy as jnp
from jax import lax
from jax.experimental import pallas as pl
from jax.experimental.pallas import tpu as pltpu

D_MODEL = 1024
GRID_W = 64
HEAD_DIM = 64
EPS = 1e-6
ROPE_THETA = 10000.0
A_HEADS = 6
A_KV_HEADS = 2
B_HEADS = 5
NA_ROWS = 8
NA_COLS = 16
C_HEADS = 5
C_NOPE = 64
C_ROPE = 32
C_V = 64
C_Q_RANK = 256
C_KV_RANK = 128
PEER_HEADS = 8
N_KEYS = 128
PEER_TOPK = 16
PEER_DKEY = 128
N_SEL = PEER_HEADS * PEER_TOPK

A_Q = A_HEADS * HEAD_DIM
A_KV = A_KV_HEADS * HEAD_DIM
B_W = B_HEADS * HEAD_DIM
C_QW = C_HEADS * 128
C_VW = C_HEADS * C_V

OFF_AQ, OFF_AK, OFF_AV = 0, 384, 512
OFF_BQ, OFF_BK, OFF_BV = 640, 1024, 1408
OFF_CQA, OFF_CKV, OFF_KPE = 1792, 2048, 2176
IN_W = 2304

NA_WIN_ROWS = NA_ROWS + 1
NA_KEYS = NA_WIN_ROWS * GRID_W
NEG_BIG = -1e30

V7X_VMEM_BYTES = 64 * 1024 * 1024
VMEM_LIMIT = 56 * 1024 * 1024

BF16 = jnp.bfloat16
F32 = jnp.float32


def _dot(a, b):
    return jnp.dot(a, b, preferred_element_type=F32)


def _dot_nt(a, b):
    return lax.dot_general(a, b, (((1,), (1,)), ((), ())), preferred_element_type=F32)


def _dot_split(a, m):
    hi = a.astype(BF16)
    lo = (a - hi.astype(F32)).astype(BF16)
    return _dot(hi, m) + _dot(lo, m)


def _rms(x):
    return x * lax.rsqrt(jnp.mean(x * x, axis=-1, keepdims=True) + EPS)


def _rope(x, cos, sin_signed, half):
    n = x.shape[1]
    fwd = pltpu.roll(x, n - half, 1)
    bwd = pltpu.roll(x, half, 1)
    lane = lax.broadcasted_iota(jnp.int32, x.shape, 1)
    partner = jnp.where((lane % (2 * half)) < half, fwd, bwd)
    return x * cos + partner * sin_signed


def _params(sem):
    return pltpu.CompilerParams(dimension_semantics=sem, vmem_limit_bytes=VMEM_LIMIT)


def _mod_kernel(c_ref, w_ref, b_ref, o_ref):
    c = c_ref[...]
    sc = (c * jax.nn.sigmoid(c)).astype(BF16)
    o_ref[0] = _dot(sc, w_ref[0].astype(BF16)) + b_ref[0]


def _modulation(c_all, ada_w, ada_b):
    depth = ada_w.shape[0]
    nb = c_all.shape[0]
    tn = 1536
    return pl.pallas_call(
        _mod_kernel,
        grid=(depth, 6 * D_MODEL // tn),
        in_specs=[
            pl.BlockSpec((nb, D_MODEL), lambda l, j: (0, 0)),
            pl.BlockSpec((1, D_MODEL, tn), lambda l, j: (l, 0, j)),
            pl.BlockSpec((1, 1, tn), lambda l, j: (l, 0, j)),
        ],
        out_specs=pl.BlockSpec((1, nb, tn), lambda l, j: (l, 0, j)),
        out_shape=jax.ShapeDtypeStruct((depth, nb, 6 * D_MODEL), F32),
        compiler_params=_params(("arbitrary", "arbitrary")),
        name="modulation",
    )(c_all, ada_w, ada_b.reshape(depth, 1, 6 * D_MODEL))


def _in_proj_kernel(x_ref, mod_ref, g1_ref, w1_ref, aqn_ref, akn_ref, cqn_ref, ckvn_ref,
                    wqb_ref, wkn_ref, wkv_ref, epl_ref, m64_ref,
                    cosa_ref, sina_ref, cosc_ref, sinc_ref, cosk_ref, sink_ref,
                    aq_ref, ak_ref, av_ref, bq_ref, bk_ref, bv_ref, cq_ref, ck_ref, cv_ref):
    x = x_ref[0]
    shift, scale = mod_ref[0:1, :], mod_ref[1:2, :]
    h = _rms(x) * g1_ref[...]
    h = h * (1.0 + scale) + shift
    z = _dot(h.astype(BF16), w1_ref[...])

    m64 = m64_ref[...]
    zq = z[:, OFF_AQ:OFF_AQ + A_Q]
    msq = _dot_split(zq * zq, m64) * (1.0 / HEAD_DIM)
    qn = zq * lax.rsqrt(msq + EPS) * aqn_ref[...]
    qn = _rope(qn, cosa_ref[...], sina_ref[...], HEAD_DIM // 4)
    aq_ref[0] = (qn * (HEAD_DIM ** -0.5)).astype(BF16)

    zk = z[:, OFF_AK:OFF_AK + A_KV]
    msk = _dot_split(zk * zk, m64[:A_KV, :A_KV]) * (1.0 / HEAD_DIM)
    kn = zk * lax.rsqrt(msk + EPS) * akn_ref[...]
    kn = _rope(kn, cosa_ref[:, :A_KV], sina_ref[:, :A_KV], HEAD_DIM // 4)
    ak_ref[0] = kn.astype(BF16)
    av_ref[0] = z[:, OFF_AV:OFF_AV + A_KV].astype(BF16)

    bq_ref[0] = (z[:, OFF_BQ:OFF_BQ + B_W] * (HEAD_DIM ** -0.5)).astype(BF16)
    bk_ref[0] = z[:, OFF_BK:OFF_BK + B_W].astype(BF16)
    bv_ref[0] = z[:, OFF_BV:OFF_BV + B_W].astype(BF16)

    cqa = _rms(z[:, OFF_CQA:OFF_CQA + C_Q_RANK]) * cqn_ref[...]
    cq = _dot(cqa.astype(BF16), wqb_ref[...])
    cq = _rope(cq, cosc_ref[...], sinc_ref[...], C_ROPE // 4)
    cq_ref[0] = (cq * ((C_NOPE + C_ROPE) ** -0.5)).astype(BF16)

    ckv = (_rms(z[:, OFF_CKV:OFF_CKV + C_KV_RANK]) * ckvn_ref[...]).astype(BF16)
    kpe = _rope(z[:, OFF_KPE:OFF_KPE + 128], cosk_ref[...], sink_ref[...], C_ROPE // 4)
    ck = _dot(ckv, wkn_ref[...]) + _dot(kpe.astype(BF16), epl_ref[...])
    ck_ref[0] = ck.astype(BF16)
    cv_ref[0] = _dot(ckv, wkv_ref[...]).astype(BF16)


def _in_proj(x, mod_l, b_off, lw, tabs, tm):
    B, S, _ = x.shape
    nt = S // tm
    tok = lambda w: pl.BlockSpec((1, tm, w), lambda b, i: (b, i, 0))
    full = lambda a: pl.BlockSpec(a.shape, lambda b, i: (0,) * a.ndim)
    tab = lambda w: pl.BlockSpec((tm, w), lambda b, i: (i, 0))
    widths = (A_Q, A_KV, A_KV, B_W, B_W, B_W, C_QW, C_QW, C_VW)
    consts = (lw["g1"], lw["w1"], lw["aqn"], lw["akn"], lw["cqn"], lw["ckvn"],
              lw["wqb"], lw["wkn"], lw["wkv"], tabs["epl"], tabs["m64"])
    return pl.pallas_call(
        _in_proj_kernel,
        grid=(B, nt),
        in_specs=[tok(D_MODEL),
                  pl.BlockSpec((None, 6, D_MODEL), lambda b, i: (b + b_off, 0, 0))]
                 + [full(a) for a in consts]
                 + [tab(A_Q), tab(A_Q), tab(C_QW), tab(C_QW), tab(128), tab(128)],
        out_specs=[tok(w) for w in widths],
        out_shape=[jax.ShapeDtypeStruct((B, S, w), BF16) for w in widths],
        compiler_params=_params(("parallel", "parallel")),
        name="in_proj",
    )(x, mod_l, *consts, tabs["cosa"], tabs["sina"], tabs["cosc"], tabs["sinc"],
      tabs["cosk"], tabs["sink"])


def _softmax_pv(s, v):
    m = jnp.max(s, axis=-1, keepdims=True)
    p = jnp.exp(s - m)
    l = jnp.sum(p, axis=-1, keepdims=True)
    return _dot(p.astype(BF16), v) / l


def _attn_a_kernel(q_ref, k_ref, v_ref, o_ref):
    group = A_HEADS // A_KV_HEADS
    outs = []
    for g in range(A_KV_HEADS):
        k = k_ref[0, :, g * HEAD_DIM:(g + 1) * HEAD_DIM]
        v = v_ref[0, :, g * HEAD_DIM:(g + 1) * HEAD_DIM]
        for j in range(group):
            h = g * group + j
            q = q_ref[0, :, h * HEAD_DIM:(h + 1) * HEAD_DIM]
            outs.append(_softmax_pv(_dot_nt(q, k), v))
    o_ref[0] = jnp.concatenate(outs, axis=-1).astype(BF16)


def _attn_c_kernel(q_ref, k_ref, v_ref, o_ref):
    outs = []
    for h in range(C_HEADS):
        q = q_ref[0, :, h * 128:(h + 1) * 128]
        k = k_ref[0, :, h * 128:(h + 1) * 128]
        v = v_ref[0, :, h * C_V:(h + 1) * C_V]
        outs.append(_softmax_pv(_dot_nt(q, k), v))
    o_ref[0] = jnp.concatenate(outs, axis=-1).astype(BF16)


def _dense_attention(kernel, q, k, v, out_w, tq, name):
    B, S, _ = q.shape
    return pl.pallas_call(
        kernel,
        grid=(B, S // tq),
        in_specs=[pl.BlockSpec((1, tq, q.shape[2]), lambda b, i: (b, i, 0)),
                  pl.BlockSpec((1, S, k.shape[2]), lambda b, i: (b, 0, 0)),
                  pl.BlockSpec((1, S, v.shape[2]), lambda b, i: (b, 0, 0))],
        out_specs=pl.BlockSpec((1, tq, out_w), lambda b, i: (b, i, 0)),
        out_shape=jax.ShapeDtypeStruct((B, S, out_w), BF16),
        compiler_params=_params(("parallel", "arbitrary")),
        name=name,
    )(q, k, v)


def _na_case(i, nblk):
    return jnp.where(i == 0, 0, jnp.where(i == 1, 1, jnp.where(i == nblk - 2, 3,
                     jnp.where(i == nblk - 1, 4, 2))))


def _attn_b_kernel(q_ref, k_ref, v_ref, bias_ref, o_ref, *, rows):
    i = pl.program_id(1)
    wstart = jnp.clip(2 * i - NA_ROWS // 2, 0, rows - NA_WIN_ROWS)
    start = pl.multiple_of(wstart * GRID_W, GRID_W)
    kw = k_ref[0, pl.ds(start, NA_KEYS), :]
    vw = v_ref[0, pl.ds(start, NA_KEYS), :]
    outs = []
    for h in range(B_HEADS):
        sl = slice(h * HEAD_DIM, (h + 1) * HEAD_DIM)
        s = _dot_nt(q_ref[0, :, sl], kw[:, sl]) + bias_ref[h]
        outs.append(_softmax_pv(s, vw[:, sl]))
    o_ref[0] = jnp.concatenate(outs, axis=-1).astype(BF16)


def _na_attention(q, k, v, bias):
    B, S, _ = q.shape
    rows = S // GRID_W
    nblk = rows // 2
    tq = 2 * GRID_W
    return pl.pallas_call(
        functools.partial(_attn_b_kernel, rows=rows),
        grid=(B, nblk),
        in_specs=[pl.BlockSpec((1, tq, B_W), lambda b, i: (b, i, 0)),
                  pl.BlockSpec((1, S, B_W), lambda b, i: (b, 0, 0)),
                  pl.BlockSpec((1, S, B_W), lambda b, i: (b, 0, 0)),
                  pl.BlockSpec((None, B_HEADS, tq, NA_KEYS),
                               lambda b, i: (_na_case(i, nblk), 0, 0, 0))],
        out_specs=pl.BlockSpec((1, tq, B_W), lambda b, i: (b, i, 0)),
        out_shape=jax.ShapeDtypeStruct((B, S, B_W), BF16),
        compiler_params=_params(("parallel", "arbitrary")),
        name="attn_b",
    )(q, k, v, bias)


def _na_bias(rpb, rows):
    assert rows >= NA_ROWS + 4 and rows % 2 == 0
    nblk = rows // 2
    wr = min(NA_ROWS, rows)
    ridx, cidx, mask = [], [], []
    for blk in (0, 1, 2, nblk - 2, nblk - 1):
        r0 = 2 * blk
        wstart = int(np.clip(r0 - NA_ROWS // 2, 0, rows - NA_WIN_ROWS))
        qq = np.arange(2 * GRID_W)
        qr, qc = r0 + qq // GRID_W, qq % GRID_W
        kk = np.arange(NA_KEYS)
        kr, kc = wstart + kk // GRID_W, kk % GRID_W
        rs = np.clip(qr - wr // 2, 0, rows - wr)
        cs = np.clip(qc - NA_COLS // 2, 0, GRID_W - NA_COLS)
        in_r = (kr[None, :] >= rs[:, None]) & (kr[None, :] < rs[:, None] + wr)
        in_c = (kc[None, :] >= cs[:, None]) & (kc[None, :] < cs[:, None] + NA_COLS)
        mask.append(in_r & in_c)
        ridx.append(np.clip(kr[None, :] - qr[:, None] + NA_ROWS - 1, 0, 2 * NA_ROWS - 2))
        cidx.append(np.clip(kc[None, :] - qc[:, None] + NA_COLS - 1, 0, 2 * NA_COLS - 2))
    ridx, cidx, mask = np.stack(ridx), np.stack(cidx), np.stack(mask)
    bias = rpb[:, ridx, cidx]
    bias = jnp.where(mask[None], bias, NEG_BIG)
    return jnp.transpose(bias, (1, 0, 2, 3)).astype(F32)


def _out_proj_kernel(oa_ref, ob_ref, oc_ref, x_ref, mod_ref, g2_ref, w_ref, x1_ref, h2_ref):
    y = (_dot(oa_ref[0], w_ref[0:A_Q, :])
         + _dot(ob_ref[0], w_ref[A_Q:A_Q + B_W, :])
         + _dot(oc_ref[0], w_ref[A_Q + B_W:, :]))
    x1 = x_ref[0] + mod_ref[2:3, :] * y
    x1_ref[0] = x1
    h2 = _rms(x1) * g2_ref[...]
    h2_ref[0] = (h2 * (1.0 + mod_ref[4:5, :]) + mod_ref[3:4, :]).astype(BF16)


def _out_proj(oa, ob, oc, x, mod_l, b_off, lw, tm):
    B, S, _ = x.shape
    tok = lambda w: pl.BlockSpec((1, tm, w), lambda b, i: (b, i, 0))
    return pl.pallas_call(
        _out_proj_kernel,
        grid=(B, S // tm),
        in_specs=[tok(A_Q), tok(B_W), tok(C_VW), tok(D_MODEL),
                  pl.BlockSpec((None, 6, D_MODEL), lambda b, i: (b + b_off, 0, 0)),
                  pl.BlockSpec((1, D_MODEL), lambda b, i: (0, 0)),
                  pl.BlockSpec((D_MODEL, D_MODEL), lambda b, i: (0, 0))],
        out_specs=[tok(D_MODEL), tok(D_MODEL)],
        out_shape=[jax.ShapeDtypeStruct((B, S, D_MODEL), F32),
                   jax.ShapeDtypeStruct((B, S, D_MODEL), BF16)],
        compiler_params=_params(("parallel", "parallel")),
        name="out_proj",
    )(oa, ob, oc, x, mod_l, lw["g2"], lw["wout"])


def _top16(s, iota, n):
    vals, idxs = [], []
    for _ in range(PEER_TOPK):
        m = jnp.max(s, axis=0, keepdims=True)
        am = jnp.min(jnp.where(s == m, iota, n), axis=0, keepdims=True)
        vals.append(m)
        idxs.append(am)
        s = jnp.where(iota == am, -jnp.inf, s)
    return jnp.concatenate(vals, axis=0), jnp.concatenate(idxs, axis=0)


def _peer_select_kernel(h2_ref, wqt_ref, sk_ref, eidx_ref, gate_ref, qt_scr, val_scr, idx_scr):
    tm = h2_ref.shape[0]
    qt_scr[...] = _dot_nt(wqt_ref[...], h2_ref[...]).astype(BF16)
    iota_k = lax.broadcasted_iota(jnp.int32, (N_KEYS, tm), 0)

    def side_body(hp, carry):
        q = qt_scr[pl.ds(pl.multiple_of(hp * (PEER_DKEY // 2), PEER_DKEY // 2), PEER_DKEY // 2), :]
        vals, idxs = _top16(_dot(sk_ref[hp], q), iota_k, N_KEYS)
        val_scr[hp] = vals
        idx_scr[hp] = idxs
        return carry

    lax.fori_loop(0, 2 * PEER_HEADS, side_body, 0)

    n_cand = PEER_TOPK * PEER_TOPK
    iota_c = lax.broadcasted_iota(jnp.int32, (n_cand, tm), 0)

    def head_body(h, carry):
        v0, v1 = val_scr[2 * h], val_scr[2 * h + 1]
        i0, i1 = idx_scr[2 * h], idx_scr[2 * h + 1]
        cand = jnp.concatenate([v0[a:a + 1] + v1 for a in range(PEER_TOPK)], axis=0)
        cidx = jnp.concatenate([i0[a:a + 1] * N_KEYS + i1 for a in range(PEER_TOPK)], axis=0)
        tv, te = [], []
        for _ in range(PEER_TOPK):
            m = jnp.max(cand, axis=0, keepdims=True)
            am = jnp.min(jnp.where(cand == m, iota_c, n_cand), axis=0, keepdims=True)
            sel = iota_c == am
            te.append(jnp.sum(jnp.where(sel, cidx, 0), axis=0, keepdims=True))
            tv.append(m)
            cand = jnp.where(sel, -jnp.inf, cand)
        tv = jnp.concatenate(tv, axis=0)
        ex = jnp.exp(tv - tv[0:1])
        row = pl.multiple_of(h * PEER_TOPK, PEER_TOPK)
        gate_ref[pl.ds(row, PEER_TOPK), :] = ex / jnp.sum(ex, axis=0, keepdims=True)
        eidx_ref[pl.ds(row, PEER_TOPK), :] = jnp.concatenate(te, axis=0) * 4
        return carry

    lax.fori_loop(0, PEER_HEADS, head_body, 0)


def _peer_select(h2, lw, tm):
    T = h2.shape[0]
    out = lambda: pl.BlockSpec((N_SEL, tm), lambda i: (0, i))
    return pl.pallas_call(
        _peer_select_kernel,
        grid=(T // tm,),
        in_specs=[pl.BlockSpec((tm, D_MODEL), lambda i: (i, 0)),
                  pl.BlockSpec((D_MODEL, D_MODEL), lambda i: (0, 0)),
                  pl.BlockSpec((2 * PEER_HEADS, N_KEYS, PEER_DKEY // 2), lambda i: (0, 0, 0))],
        out_specs=[out(), out()],
        out_shape=[jax.ShapeDtypeStruct((N_SEL, T), jnp.int32),
                   jax.ShapeDtypeStruct((N_SEL, T), F32)],
        scratch_shapes=[pltpu.VMEM((D_MODEL, tm), BF16),
                        pltpu.VMEM((2 * PEER_HEADS, PEER_TOPK, tm), F32),
                        pltpu.VMEM((2 * PEER_HEADS, PEER_TOPK, tm), jnp.int32)],
        compiler_params=_params(("parallel",)),
        name="peer_select",
    )(h2, lw["wqt"], lw["sk"])


ROW_WORDS = D_MODEL // 2 // 128
GATHER_STRIDE = N_SEL + 1
TOK_GROUP = 16


def _gather_rows(idx_ref, t, tab_ref, tile_ref):
    for k in range(N_SEL):
        e4 = pl.multiple_of(idx_ref[t, k], ROW_WORDS)
        tile_ref[pl.ds(k, ROW_WORDS, stride=GATHER_STRIDE), :] = tab_ref[pl.ds(e4, ROW_WORDS), :]


def _tile_chunk(tile_ref, j):
    return pltpu.bitcast(tile_ref[pl.ds(j * GATHER_STRIDE, N_SEL), :], BF16)


def _peer_u_kernel(idx_ref, h2_ref, gate_ref, tab_ref, ssum_ref, pa_ref, pb_ref, w_ref,
                   tile_scr, comb_scr):
    tm = h2_ref.shape[0]
    row_iota = lax.broadcasted_iota(jnp.int32, (TOK_GROUP, D_MODEL), 0)
    lane = lax.broadcasted_iota(jnp.int32, (TOK_GROUP, 2 * N_SEL), 1)

    def group_body(g, carry):
        base = pl.multiple_of(g * TOK_GROUP, TOK_GROUP)
        xs = h2_ref[pl.ds(base, TOK_GROUP), :].astype(F32)

        def tok_body(i, acc):
            _gather_rows(idx_ref, base + i, tab_ref, tile_scr)
            w = jnp.concatenate([_tile_chunk(tile_scr, j) for j in range(ROW_WORDS)], axis=-1)
            xm = jnp.where(row_iota == i, xs, 0.0)
            lhs = jnp.concatenate([xm[:, :D_MODEL // 2], xm[:, D_MODEL // 2:]], axis=0)
            return acc + _dot_nt(lhs.astype(BF16), w)

        acc = lax.fori_loop(0, TOK_GROUP, tok_body, jnp.zeros((2 * TOK_GROUP, 2 * N_SEL), F32))
        comb_scr[pl.ds(base, TOK_GROUP), :] = jnp.where(lane % 2 == 0, acc[:TOK_GROUP], acc[TOK_GROUP:])
        return carry

    lax.fori_loop(0, tm // TOK_GROUP, group_body, 0)
    a = _dot_split(comb_scr[...], ssum_ref[...])
    act = 0.5 * a * (1.0 + lax.erf(a * (2.0 ** -0.5)))
    w = (gate_ref[...] * act).astype(BF16)
    w_ref[:, :2 * N_SEL] = _dot(w, pa_ref[...]).astype(BF16)
    w_ref[:, 2 * N_SEL:] = _dot(w, pb_ref[...]).astype(BF16)


def _peer_v_kernel(idx_ref, w_ref, x1_ref, mod_ref, fg_ref, tab_ref, o_ref, tile_scr, *, final):
    tm = w_ref.shape[0]
    row_iota = lax.broadcasted_iota(jnp.int32, (TOK_GROUP, 4 * N_SEL), 0)
    g2 = mod_ref[5:6, :]

    def group_body(g, carry):
        base = pl.multiple_of(g * TOK_GROUP, TOK_GROUP)
        ws = w_ref[pl.ds(base, TOK_GROUP), :].astype(F32)

        def tok_body(i, accs):
            _gather_rows(idx_ref, base + i, tab_ref, tile_scr)
            wm = jnp.where(row_iota == i, ws, 0.0)
            lhs = jnp.concatenate([wm[:, :2 * N_SEL], wm[:, 2 * N_SEL:]], axis=0).astype(BF16)
            return tuple(acc + _dot(lhs, _tile_chunk(tile_scr, j)) for j, acc in enumerate(accs))

        zero = jnp.zeros((2 * TOK_GROUP, 128), F32)
        accs = lax.fori_loop(0, TOK_GROUP, tok_body, (zero,) * ROW_WORDS)
        y = jnp.concatenate([a[:TOK_GROUP] for a in accs] + [a[TOK_GROUP:] for a in accs], axis=-1)
        x2 = x1_ref[pl.ds(base, TOK_GROUP), :] + g2 * y
        if final:
            x2 = _rms(x2) * fg_ref[...]
        o_ref[pl.ds(base, TOK_GROUP), :] = x2
        return carry

    lax.fori_loop(0, tm // TOK_GROUP, group_body, 0)


def _table_spec(tab):
    return pl.BlockSpec(tab.shape, lambda i: (0, 0), pipeline_mode=pl.Buffered(1))


def _peer_u(eidx, h2, gate, utab, tabs, tm):
    T = h2.shape[0]
    return pl.pallas_call(
        _peer_u_kernel,
        grid=(T // tm,),
        in_specs=[pl.BlockSpec((tm, N_SEL), lambda i: (i, 0), memory_space=pltpu.SMEM),
                  pl.BlockSpec((tm, D_MODEL), lambda i: (i, 0)),
                  pl.BlockSpec((tm, N_SEL), lambda i: (i, 0)),
                  _table_spec(utab),
                  pl.BlockSpec((2 * N_SEL, N_SEL), lambda i: (0, 0)),
                  pl.BlockSpec((N_SEL, 2 * N_SEL), lambda i: (0, 0)),
                  pl.BlockSpec((N_SEL, 2 * N_SEL), lambda i: (0, 0))],
        out_specs=pl.BlockSpec((tm, 4 * N_SEL), lambda i: (i, 0)),
        out_shape=jax.ShapeDtypeStruct((T, 4 * N_SEL), BF16),
        scratch_shapes=[pltpu.VMEM((ROW_WORDS * GATHER_STRIDE, 128), jnp.uint32),
                        pltpu.VMEM((tm, 2 * N_SEL), F32)],
        compiler_params=_params(("arbitrary",)),
        name="peer_u",
    )(eidx, h2, gate, utab, tabs["ssum"], tabs["pa"], tabs["pb"])


def _peer_v(eidx, w, x1, mod_l, b_idx, final_g, vtab, tm, final):
    T = w.shape[0]
    tiles_per_batch = T // tm // b_idx[1]
    return pl.pallas_call(
        functools.partial(_peer_v_kernel, final=final),
        grid=(T // tm,),
        in_specs=[pl.BlockSpec((tm, N_SEL), lambda i: (i, 0), memory_space=pltpu.SMEM),
                  pl.BlockSpec((tm, 4 * N_SEL), lambda i: (i, 0)),
                  pl.BlockSpec((tm, D_MODEL), lambda i: (i, 0)),
                  pl.BlockSpec((None, 6, D_MODEL),
                               lambda i: (i // tiles_per_batch + b_idx[0], 0, 0)),
                  pl.BlockSpec((1, D_MODEL), lambda i: (0, 0)),
                  _table_spec(vtab)],
        out_specs=pl.BlockSpec((tm, D_MODEL), lambda i: (i, 0)),
        out_shape=jax.ShapeDtypeStruct((T, D_MODEL), F32),
        scratch_shapes=[pltpu.VMEM((ROW_WORDS * GATHER_STRIDE, 128), jnp.uint32)],
        compiler_params=_params(("arbitrary",)),
        name="peer_v",
    )(eidx, w, x1, mod_l, final_g, vtab)


def _pack_table(t):
    n = t.shape[0]
    b = lax.bitcast_convert_type(t.astype(BF16), jnp.uint16).astype(jnp.uint32)
    words = b[:, :D_MODEL // 2] | (b[:, D_MODEL // 2:] << 16)
    return words.reshape(n * ROW_WORDS, 128)


def _rope_lane_tables(S, dim):
    dq = dim // 4
    freqs = ROPE_THETA ** (-jnp.arange(dq, dtype=F32) / dq)
    t = jnp.arange(S, dtype=jnp.int32)
    row = (t // GRID_W).astype(F32)
    col = (t % GRID_W).astype(F32)
    ang = jnp.stack([row[:, None] * freqs, col[:, None] * freqs], axis=1)
    cos, sin = jnp.cos(ang), jnp.sin(ang)
    d = np.arange(dim)
    blk, f = d // (2 * dq), d % dq
    sign = np.where((d % (2 * dq)) < dq, -1.0, 1.0).astype(np.float32)
    return cos[:, blk, f], sin[:, blk, f] * sign


def _const_tables(S):
    cos_a, sin_a = _rope_lane_tables(S, HEAD_DIM)
    cos_c, sin_c = _rope_lane_tables(S, C_ROPE)
    one = jnp.ones((S, 128), F32)
    zero = jnp.zeros((S, 128), F32)
    cosk = one.at[:, :C_ROPE].set(cos_c)
    sink = zero.at[:, :C_ROPE].set(sin_c)
    cosh = one.at[:, C_NOPE:C_NOPE + C_ROPE].set(cos_c)
    sinh = zero.at[:, C_NOPE:C_NOPE + C_ROPE].set(sin_c)
    head_of = np.arange(A_Q) // HEAD_DIM
    epl = np.zeros((128, C_QW), np.float32)
    for h in range(C_HEADS):
        epl[np.arange(C_ROPE), h * 128 + C_NOPE + np.arange(C_ROPE)] = 1.0
    k = np.arange(N_SEL)
    ssum = np.zeros((2 * N_SEL, N_SEL), np.float32)
    ssum[2 * k, k] = 1.0
    ssum[2 * k + 1, k] = 1.0
    pa = np.zeros((N_SEL, 2 * N_SEL), np.float32)
    pb = np.zeros((N_SEL, 2 * N_SEL), np.float32)
    pa[k, 2 * k] = 1.0
    pb[k, 2 * k + 1] = 1.0
    return {
        "cosa": jnp.tile(cos_a, (1, A_HEADS)), "sina": jnp.tile(sin_a, (1, A_HEADS)),
        "cosc": jnp.tile(cosh, (1, C_HEADS)), "sinc": jnp.tile(sinh, (1, C_HEADS)),
        "cosk": cosk, "sink": sink,
        "m64": jnp.asarray(head_of[:, None] == head_of[None, :], BF16),
        "epl": jnp.asarray(epl, BF16), "ssum": jnp.asarray(ssum, BF16),
        "pa": jnp.asarray(pa, BF16), "pb": jnp.asarray(pb, BF16),
    }


def _layer_weights(l, norm1_g, norm2_g, w_in, a_q_norm, a_k_norm, c_q_norm, c_wqb, c_kv_norm, c_wkvb,
                   w_out, peer_wq, peer_subkeys, peer_u, peer_v):
    src = np.cumsum((0, A_Q, A_KV, A_KV, B_W, B_W, B_W, C_Q_RANK, C_KV_RANK))
    dst = (OFF_AQ, OFF_AK, OFF_AV, OFF_BQ, OFF_BK, OFF_BV, OFF_CQA, OFF_CKV, OFF_KPE)
    wid = (A_Q, A_KV, A_KV, B_W, B_W, B_W, C_Q_RANK, C_KV_RANK, C_ROPE)
    w1 = jnp.zeros((D_MODEL, IN_W), F32)
    for s, d, w in zip(src, dst, wid):
        w1 = w1.at[:, d:d + w].set(w_in[l][:, s:s + w])
    wqb = jnp.zeros((C_Q_RANK, C_HEADS, 128), F32).at[:, :, :C_NOPE + C_ROPE].set(
        c_wqb[l].reshape(C_Q_RANK, C_HEADS, C_NOPE + C_ROPE)).reshape(C_Q_RANK, C_QW)
    wkvb = c_wkvb[l].reshape(C_KV_RANK, C_HEADS, C_NOPE + C_V)
    wkn = jnp.zeros((C_KV_RANK, C_HEADS, 128), F32).at[:, :, :C_NOPE].set(
        wkvb[:, :, :C_NOPE]).reshape(C_KV_RANK, C_QW)
    wkv = wkvb[:, :, C_NOPE:].reshape(C_KV_RANK, C_VW)
    return {
        "g1": norm1_g[l][None, :], "g2": norm2_g[l][None, :],
        "w1": w1.astype(BF16),
        "aqn": jnp.tile(a_q_norm[l], A_HEADS)[None, :], "akn": jnp.tile(a_k_norm[l], A_KV_HEADS)[None, :],
        "cqn": c_q_norm[l][None, :], "ckvn": c_kv_norm[l][None, :],
        "wqb": wqb.astype(BF16), "wkn": wkn.astype(BF16), "wkv": wkv.astype(BF16),
        "wout": w_out[l].astype(BF16),
        "wqt": peer_wq[l].T.astype(BF16),
        "sk": peer_subkeys[l].reshape(2 * PEER_HEADS, N_KEYS, PEER_DKEY // 2).astype(BF16),
        "utab": _pack_table(peer_u[l]), "vtab": _pack_table(peer_v[l]),
    }


def _tile(n, pref):
    t = pref
    while n % t:
        t //= 2
    return t


def _trunk(x, mod, b_off, n_batch_total, layers, na_bias, final_g):
    B, S, _ = x.shape
    T = B * S
    tabs = _const_tables(S)
    tm = _tile(S, 256)
    tq = _tile(S, 256)
    tp = _tile(S, 256)
    depth = len(layers)
    for l, lw in enumerate(layers):
        aq, ak, av, bq, bk, bv, cq, ck, cv = _in_proj(x, mod[l], b_off, lw, tabs, tm)
        oa = _dense_attention(_attn_a_kernel, aq, ak, av, A_Q, tq, "attn_a")
        ob = _na_attention(bq, bk, bv, na_bias[l])
        oc = _dense_attention(_attn_c_kernel, cq, ck, cv, C_VW, tq, "attn_c")
        x1, h2 = _out_proj(oa, ob, oc, x, mod[l], b_off, lw, tm)
        h2 = h2.reshape(T, D_MODEL)
        eidx_t, gate_t = _peer_select(h2, lw, tp)
        eidx, gate = eidx_t.T, gate_t.T
        w = _peer_u(eidx, h2, gate, lw["utab"], tabs, tp)
        x = _peer_v(eidx, w, x1.reshape(T, D_MODEL), mod[l], (b_off, B), final_g[None, :],
                    lw["vtab"], tp, l == depth - 1).reshape(B, S, D_MODEL)
    return x


def kernel(x_prompt, x_sample, c_prompt, c_sample, ada_w, ada_b, norm1_g, norm2_g, w_in, a_q_norm, a_k_norm, b_rpb, c_q_norm, c_wqb, c_kv_norm, c_wkvb, w_out, peer_wq, peer_subkeys, peer_u, peer_v, final_g):
    depth = ada_w.shape[0]
    c_all = jnp.concatenate([c_prompt, c_sample], axis=0)
    mod = _modulation(c_all, ada_w, ada_b).reshape(depth, c_all.shape[0], 6, D_MODEL)
    layers = [_layer_weights(l, norm1_g, norm2_g, w_in, a_q_norm, a_k_norm, c_q_norm, c_wqb, c_kv_norm,
                             c_wkvb, w_out, peer_wq, peer_subkeys, peer_u, peer_v) for l in range(depth)]
    outs = []
    b_off = 0
    for x in (x_prompt, x_sample):
        rows = x.shape[1] // GRID_W
        na_bias = [_na_bias(b_rpb[l], rows) for l in range(depth)]
        outs.append(_trunk(x, mod, b_off, c_all.shape[0], layers, na_bias, final_g))
        b_off += x.shape[0]
    return tuple(outs)
```

```python
import functools

import numpy as np
import jax
import jax.numpy as jnp
from jax import lax
from jax.experimental import pallas as pl
from jax.experimental.pallas import tpu as pltpu

D_MODEL = 1024
GRID_W = 64
HEAD_DIM = 64
EPS = 1e-6
ROPE_THETA = 10000.0
A_HEADS = 6
A_KV_HEADS = 2
B_HEADS = 5
NA_ROWS = 8
NA_COLS = 16
C_HEADS = 5
C_NOPE = 64
C_ROPE = 32
C_V = 64
C_Q_RANK = 256
C_KV_RANK = 128
PEER_HEADS = 8
N_KEYS = 128
PEER_TOPK = 16
PEER_DKEY = 128
N_SEL = PEER_HEADS * PEER_TOPK

A_Q = A_HEADS * HEAD_DIM
A_KV = A_KV_HEADS * HEAD_DIM
B_W = B_HEADS * HEAD_DIM
C_QW = C_HEADS * 128
C_VW = C_HEADS * C_V

OFF_AQ, OFF_AK, OFF_AV = 0, 384, 512
OFF_BQ, OFF_BK, OFF_BV = 640, 1024, 1408
OFF_CQA, OFF_CKV, OFF_KPE = 1792, 2048, 2176
IN_W = 2304

NA_WIN_ROWS = NA_ROWS + 1
NA_KEYS = NA_WIN_ROWS * GRID_W
NEG_BIG = -1e30

V7X_VMEM_BYTES = 64 * 1024 * 1024
VMEM_LIMIT = 56 * 1024 * 1024

BF16 = jnp.bfloat16
F32 = jnp.float32


def _dot(a, b):
    return jnp.dot(a, b, preferred_element_type=F32)


def _dot_nt(a, b):
    return lax.dot_general(a, b, (((1,), (1,)), ((), ())), preferred_element_type=F32)


def _dot_split(a, m):
    hi = a.astype(BF16)
    lo = (a - hi.astype(F32)).astype(BF16)
    return _dot(hi, m) + _dot(lo, m)


def _rms(x):
    return x * lax.rsqrt(jnp.mean(x * x, axis=-1, keepdims=True) + EPS)


def _rope(x, cos, sin_signed, half):
    n = x.shape[1]
    fwd = pltpu.roll(x, n - half, 1)
    bwd = pltpu.roll(x, half, 1)
    lane = lax.broadcasted_iota(jnp.int32, x.shape, 1)
    partner = jnp.where((lane % (2 * half)) < half, fwd, bwd)
    return x * cos + partner * sin_signed


def _params(sem):
    return pltpu.CompilerParams(dimension_semantics=sem, vmem_limit_bytes=VMEM_LIMIT)


def _mod_kernel(c_ref, w_ref, b_ref, o_ref):
    c = c_ref[...]
    sc = (c * jax.nn.sigmoid(c)).astype(BF16)
    o_ref[0] = _dot(sc, w_ref[0].astype(BF16)) + b_ref[0]


def _modulation(c_all, ada_w, ada_b):
    depth = ada_w.shape[0]
    nb = c_all.shape[0]
    tn = 1536
    return pl.pallas_call(
        _mod_kernel,
        grid=(depth, 6 * D_MODEL // tn),
        in_specs=[
            pl.BlockSpec((nb, D_MODEL), lambda l, j: (0, 0)),
            pl.BlockSpec((1, D_MODEL, tn), lambda l, j: (l, 0, j)),
            pl.BlockSpec((1, 1, tn), lambda l, j: (l, 0, j)),
        ],
        out_specs=pl.BlockSpec((1, nb, tn), lambda l, j: (l, 0, j)),
        out_shape=jax.ShapeDtypeStruct((depth, nb, 6 * D_MODEL), F32),
        compiler_params=_params(("arbitrary", "arbitrary")),
        name="modulation",
    )(c_all, ada_w, ada_b.reshape(depth, 1, 6 * D_MODEL))


def _in_proj_kernel(x_ref, mod_ref, g1_ref, w1_ref, aqn_ref, akn_ref, cqn_ref, ckvn_ref,
                    wqb_ref, wkn_ref, wkv_ref, epl_ref, m64_ref,
                    cosa_ref, sina_ref, cosc_ref, sinc_ref, cosk_ref, sink_ref,
                    aq_ref, ak_ref, av_ref, bq_ref, bk_ref, bv_ref, cq_ref, ck_ref, cv_ref):
    x = x_ref[0]
    shift, scale = mod_ref[0:1, :], mod_ref[1:2, :]
    h = _rms(x) * g1_ref[...]
    h = h * (1.0 + scale) + shift
    z = _dot(h.astype(BF16), w1_ref[...])

    m64 = m64_ref[...]
    zq = z[:, OFF_AQ:OFF_AQ + A_Q]
    msq = _dot_split(zq * zq, m64) * (1.0 / HEAD_DIM)
    qn = zq * lax.rsqrt(msq + EPS) * aqn_ref[...]
    qn = _rope(qn, cosa_ref[...], sina_ref[...], HEAD_DIM // 4)
    aq_ref[0] = (qn * (HEAD_DIM ** -0.5)).astype(BF16)

    zk = z[:, OFF_AK:OFF_AK + A_KV]
    msk = _dot_split(zk * zk, m64[:A_KV, :A_KV]) * (1.0 / HEAD_DIM)
    kn = zk * lax.rsqrt(msk + EPS) * akn_ref[...]
    kn = _rope(kn, cosa_ref[:, :A_KV], sina_ref[:, :A_KV], HEAD_DIM // 4)
    ak_ref[0] = kn.astype(BF16)
    av_ref[0] = z[:, OFF_AV:OFF_AV + A_KV].astype(BF16)

    bq_ref[0] = (z[:, OFF_BQ:OFF_BQ + B_W] * (HEAD_DIM ** -0.5)).astype(BF16)
    bk_ref[0] = z[:, OFF_BK:OFF_BK + B_W].astype(BF16)
    bv_ref[0] = z[:, OFF_BV:OFF_BV + B_W].astype(BF16)

    cqa = _rms(z[:, OFF_CQA:OFF_CQA + C_Q_RANK]) * cqn_ref[...]
    cq = _dot(cqa.astype(BF16), wqb_ref[...])
    cq = _rope(cq, cosc_ref[...], sinc_ref[...], C_ROPE // 4)
    cq_ref[0] = (cq * ((C_NOPE + C_ROPE) ** -0.5)).astype(BF16)

    ckv = (_rms(z[:, OFF_CKV:OFF_CKV + C_KV_RANK]) * ckvn_ref[...]).astype(BF16)
    kpe = _rope(z[:, OFF_KPE:OFF_KPE + 128], cosk_ref[...], sink_ref[...], C_ROPE // 4)
    ck = _dot(ckv, wkn_ref[...]) + _dot(kpe.astype(BF16), epl_ref[...])
    ck_ref[0] = ck.astype(BF16)
    cv_ref[0] = _dot(ckv, wkv_ref[...]).astype(BF16)


def _in_proj(x, mod_l, b_off, lw, tabs, tm):
    B, S, _ = x.shape
    nt = S // tm
    tok = lambda w: pl.BlockSpec((1, tm, w), lambda b, i: (b, i, 0))
    full = lambda a: pl.BlockSpec(a.shape, lambda b, i: (0,) * a.ndim)
    tab = lambda w: pl.BlockSpec((tm, w), lambda b, i: (i, 0))
    widths = (A_Q, A_KV, A_KV, B_W, B_W, B_W, C_QW, C_QW, C_VW)
    consts = (lw["g1"], lw["w1"], lw["aqn"], lw["akn"], lw["cqn"], lw["ckvn"],
              lw["wqb"], lw["wkn"], lw["wkv"], tabs["epl"], tabs["m64"])
    return pl.pallas_call(
        _in_proj_kernel,
        grid=(B, nt),
        in_specs=[tok(D_MODEL),
                  pl.BlockSpec((None, 6, D_MODEL), lambda b, i: (b + b_off, 0, 0))]
                 + [full(a) for a in consts]
                 + [tab(A_Q), tab(A_Q), tab(C_QW), tab(C_QW), tab(128), tab(128)],
        out_specs=[tok(w) for w in widths],
        out_shape=[jax.ShapeDtypeStruct((B, S, w), BF16) for w in widths],
        compiler_params=_params(("parallel", "parallel")),
        name="in_proj",
    )(x, mod_l, *consts, tabs["cosa"], tabs["sina"], tabs["cosc"], tabs["sinc"],
      tabs["cosk"], tabs["sink"])


def _softmax_pv(s, v):
    m = jnp.max(s, axis=-1, keepdims=True)
    p = jnp.exp(s - m)
    l = jnp.sum(p, axis=-1, keepdims=True)
    return _dot(p.astype(BF16), v) / l


def _attn_a_kernel(q_ref, k_ref, v_ref, o_ref):
    group = A_HEADS // A_KV_HEADS
    outs = []
    for g in range(A_KV_HEADS):
        k = k_ref[0, :, g * HEAD_DIM:(g + 1) * HEAD_DIM]
        v = v_ref[0, :, g * HEAD_DIM:(g + 1) * HEAD_DIM]
        for j in range(group):
            h = g * group + j
            q = q_ref[0, :, h * HEAD_DIM:(h + 1) * HEAD_DIM]
            outs.append(_softmax_pv(_dot_nt(q, k), v))
    o_ref[0] = jnp.concatenate(outs, axis=-1).astype(BF16)


def _attn_c_kernel(q_ref, k_ref, v_ref, o_ref):
    outs = []
    for h in range(C_HEADS):
        q = q_ref[0, :, h * 128:(h + 1) * 128]
        k = k_ref[0, :, h * 128:(h + 1) * 128]
        v = v_ref[0, :, h * C_V:(h + 1) * C_V]
        outs.append(_softmax_pv(_dot_nt(q, k), v))
    o_ref[0] = jnp.concatenate(outs, axis=-1).astype(BF16)


def _dense_attention(kernel, q, k, v, out_w, tq, name):
    B, S, _ = q.shape
    return pl.pallas_call(
        kernel,
        grid=(B, S // tq),
        in_specs=[pl.BlockSpec((1, tq, q.shape[2]), lambda b, i: (b, i, 0)),
                  pl.BlockSpec((1, S, k.shape[2]), lambda b, i: (b, 0, 0)),
                  pl.BlockSpec((1, S, v.shape[2]), lambda b, i: (b, 0, 0))],
        out_specs=pl.BlockSpec((1, tq, out_w), lambda b, i: (b, i, 0)),
        out_shape=jax.ShapeDtypeStruct((B, S, out_w), BF16),
        compiler_params=_params(("parallel", "arbitrary")),
        name=name,
    )(q, k, v)


def _na_case(i, nblk):
    return jnp.where(i == 0, 0, jnp.where(i == 1, 1, jnp.where(i == nblk - 2, 3,
                     jnp.where(i == nblk - 1, 4, 2))))


def _attn_b_kernel(q_ref, k_ref, v_ref, bias_ref, o_ref, *, rows):
    i = pl.program_id(1)
    wstart = jnp.clip(2 * i - NA_ROWS // 2, 0, rows - NA_WIN_ROWS)
    start = pl.multiple_of(wstart * GRID_W, GRID_W)
    kw = k_ref[0, pl.ds(start, NA_KEYS), :]
    vw = v_ref[0, pl.ds(start, NA_KEYS), :]
    outs = []
    for h in range(B_HEADS):
        sl = slice(h * HEAD_DIM, (h + 1) * HEAD_DIM)
        s = _dot_nt(q_ref[0, :, sl], kw[:, sl]) + bias_ref[h]
        outs.append(_softmax_pv(s, vw[:, sl]))
    o_ref[0] = jnp.concatenate(outs, axis=-1).astype(BF16)


def _na_attention(q, k, v, bias):
    B, S, _ = q.shape
    rows = S // GRID_W
    nblk = rows // 2
    tq = 2 * GRID_W
    return pl.pallas_call(
        functools.partial(_attn_b_kernel, rows=rows),
        grid=(B, nblk),
        in_specs=[pl.BlockSpec((1, tq, B_W), lambda b, i: (b, i, 0)),
                  pl.BlockSpec((1, S, B_W), lambda b, i: (b, 0, 0)),
                  pl.BlockSpec((1, S, B_W), lambda b, i: (b, 0, 0)),
                  pl.BlockSpec((None, B_HEADS, tq, NA_KEYS),
                               lambda b, i: (_na_case(i, nblk), 0, 0, 0))],
        out_specs=pl.BlockSpec((1, tq, B_W), lambda b, i: (b, i, 0)),
        out_shape=jax.ShapeDtypeStruct((B, S, B_W), BF16),
        compiler_params=_params(("parallel", "arbitrary")),
        name="attn_b",
    )(q, k, v, bias)


def _na_bias(rpb, rows):
    assert rows >= NA_ROWS + 4 and rows % 2 == 0
    nblk = rows // 2
    wr = min(NA_ROWS, rows)
    qc = np.arange(GRID_W)
    cs = np.clip(qc - NA_COLS // 2, 0, GRID_W - NA_COLS)
    in_c = (qc[None, :] >= cs[:, None]) & (qc[None, :] < cs[:, None] + NA_COLS)
    cidx = qc[None, :] - qc[:, None] + NA_COLS - 1
    toeplitz = (cidx[None] == np.arange(2 * NA_COLS - 1)[:, None, None]) & in_c[None]
    ridx = np.zeros((5, 2, NA_WIN_ROWS), np.int64)
    in_r = np.zeros((5, 2, NA_WIN_ROWS), bool)
    for case, blk in enumerate((0, 1, 2, nblk - 2, nblk - 1)):
        wstart = int(np.clip(2 * blk - NA_ROWS // 2, 0, rows - NA_WIN_ROWS))
        for j in range(2):
            qr = 2 * blk + j
            rs = int(np.clip(qr - wr // 2, 0, rows - wr))
            kr = wstart + np.arange(NA_WIN_ROWS)
            in_r[case, j] = (kr >= rs) & (kr < rs + wr)
            ridx[case, j] = np.clip(kr - qr + NA_ROWS - 1, 0, 2 * NA_ROWS - 2)
    picked = rpb[:, ridx, :]
    bias = jnp.einsum("hcjkd,dxy->chjxky", picked, jnp.asarray(toeplitz, F32),
                      precision=lax.Precision.HIGHEST)
    mask = in_r[:, None, :, None, :, None] & in_c[None, None, None, :, None, :]
    bias = jnp.where(mask, bias, NEG_BIG)
    return bias.reshape(5, rpb.shape[0], 2 * GRID_W, NA_KEYS)


def _out_proj_kernel(oa_ref, ob_ref, oc_ref, x_ref, mod_ref, g2_ref, w_ref, x1_ref, h2_ref):
    y = (_dot(oa_ref[0], w_ref[0:A_Q, :])
         + _dot(ob_ref[0], w_ref[A_Q:A_Q + B_W, :])
         + _dot(oc_ref[0], w_ref[A_Q + B_W:, :]))
    x1 = x_ref[0] + mod_ref[2:3, :] * y
    x1_ref[0] = x1
    h2 = _rms(x1) * g2_ref[...]
    h2_ref[0] = (h2 * (1.0 + mod_ref[4:5, :]) + mod_ref[3:4, :]).astype(BF16)


def _out_proj(oa, ob, oc, x, mod_l, b_off, lw, tm):
    B, S, _ = x.shape
    tok = lambda w: pl.BlockSpec((1, tm, w), lambda b, i: (b, i, 0))
    return pl.pallas_call(
        _out_proj_kernel,
        grid=(B, S // tm),
        in_specs=[tok(A_Q), tok(B_W), tok(C_VW), tok(D_MODEL),
                  pl.BlockSpec((None, 6, D_MODEL), lambda b, i: (b + b_off, 0, 0)),
                  pl.BlockSpec((1, D_MODEL), lambda b, i: (0, 0)),
                  pl.BlockSpec((D_MODEL, D_MODEL), lambda b, i: (0, 0))],
        out_specs=[tok(D_MODEL), tok(D_MODEL)],
        out_shape=[jax.ShapeDtypeStruct((B, S, D_MODEL), F32),
                   jax.ShapeDtypeStruct((B, S, D_MODEL), BF16)],
        compiler_params=_params(("parallel", "parallel")),
        name="out_proj",
    )(oa, ob, oc, x, mod_l, lw["g2"], lw["wout"])


def _top16(s, iota, n):
    vals, idxs = [], []
    for _ in range(PEER_TOPK):
        m = jnp.max(s, axis=0, keepdims=True)
        am = jnp.min(jnp.where(s == m, iota, n), axis=0, keepdims=True)
        vals.append(m)
        idxs.append(am)
        s = jnp.where(iota == am, -jnp.inf, s)
    return jnp.concatenate(vals, axis=0), jnp.concatenate(idxs, axis=0)


def _peer_select_kernel(h2_ref, wqt_ref, sk_ref, eidx_ref, gate_ref, qt_scr, val_scr, idx_scr):
    tm = h2_ref.shape[0]
    qt_scr[...] = _dot_nt(wqt_ref[...], h2_ref[...]).astype(BF16)
    iota_k = lax.broadcasted_iota(jnp.int32, (N_KEYS, tm), 0)

    def side_body(hp, carry):
        q = qt_scr[pl.ds(pl.multiple_of(hp * (PEER_DKEY // 2), PEER_DKEY // 2), PEER_DKEY // 2), :]
        vals, idxs = _top16(_dot(sk_ref[hp], q), iota_k, N_KEYS)
        val_scr[hp] = vals
        idx_scr[hp] = idxs
        return carry

    lax.fori_loop(0, 2 * PEER_HEADS, side_body, 0)

    half = PEER_TOPK // 2
    iota8 = lax.broadcasted_iota(jnp.int32, (half, tm), 0)
    flat = jnp.concatenate([iota8 * PEER_TOPK + b for b in range(half)]
                           + [iota8 + half, (iota8 + half) * PEER_TOPK], axis=0)
    n_flat = PEER_TOPK * PEER_TOPK

    def head_body(h, carry):
        v0, v1 = val_scr[2 * h], val_scr[2 * h + 1]
        i0, i1 = idx_scr[2 * h], idx_scr[2 * h + 1]
        cand, cidx = [], []
        for b in range(half):
            c = v0[:half] + v1[b:b + 1]
            a_max = PEER_TOPK // (b + 1) - 1
            cand.append(c if a_max >= half - 1 else jnp.where(iota8 <= a_max, c, -jnp.inf))
            cidx.append(i0[:half] * N_KEYS + i1[b:b + 1])
        cand += [v0[0:1] + v1[half:], v0[half:] + v1[0:1]]
        cidx += [i0[0:1] * N_KEYS + i1[half:], i0[half:] * N_KEYS + i1[0:1]]
        cand = jnp.concatenate(cand, axis=0)
        cidx = jnp.concatenate(cidx, axis=0)
        tv, te = [], []
        for _ in range(PEER_TOPK):
            m = jnp.max(cand, axis=0, keepdims=True)
            am = jnp.min(jnp.where(cand == m, flat, n_flat), axis=0, keepdims=True)
            sel = flat == am
            te.append(jnp.sum(jnp.where(sel, cidx, 0), axis=0, keepdims=True))
            tv.append(m)
            cand = jnp.where(sel, -jnp.inf, cand)
        tv = jnp.concatenate(tv, axis=0)
        ex = jnp.exp(tv - tv[0:1])
        row = pl.multiple_of(h * PEER_TOPK, PEER_TOPK)
        gate_ref[pl.ds(row, PEER_TOPK), :] = ex / jnp.sum(ex, axis=0, keepdims=True)
        eidx_ref[pl.ds(row, PEER_TOPK), :] = jnp.concatenate(te, axis=0) * 4
        return carry

    lax.fori_loop(0, PEER_HEADS, head_body, 0)


def _peer_select(h2, lw, tm):
    T = h2.shape[0]
    out = lambda: pl.BlockSpec((N_SEL, tm), lambda i: (0, i))
    return pl.pallas_call(
        _peer_select_kernel,
        grid=(T // tm,),
        in_specs=[pl.BlockSpec((tm, D_MODEL), lambda i: (i, 0)),
                  pl.BlockSpec((D_MODEL, D_MODEL), lambda i: (0, 0)),
                  pl.BlockSpec((2 * PEER_HEADS, N_KEYS, PEER_DKEY // 2), lambda i: (0, 0, 0))],
        out_specs=[out(), out()],
        out_shape=[jax.ShapeDtypeStruct((N_SEL, T), jnp.int32),
                   jax.ShapeDtypeStruct((N_SEL, T), F32)],
        scratch_shapes=[pltpu.VMEM((D_MODEL, tm), BF16),
                        pltpu.VMEM((2 * PEER_HEADS, PEER_TOPK, tm), F32),
                        pltpu.VMEM((2 * PEER_HEADS, PEER_TOPK, tm), jnp.int32)],
        compiler_params=_params(("parallel",)),
        name="peer_select",
    )(h2, lw["wqt"], lw["sk"])


ROW_WORDS = D_MODEL // 2 // 128
GATHER_STRIDE = N_SEL + 1
TOK_GROUP = 16


def _gather_rows(idx_ref, t, tab_ref, tile_ref):
    for k in range(N_SEL):
        e4 = pl.multiple_of(idx_ref[t, k], ROW_WORDS)
        tile_ref[pl.ds(k, ROW_WORDS, stride=GATHER_STRIDE), :] = tab_ref[pl.ds(e4, ROW_WORDS), :]


def _tile_chunk(tile_ref, j):
    return pltpu.bitcast(tile_ref[pl.ds(j * GATHER_STRIDE, N_SEL), :], BF16)


def _peer_u_kernel(idx_ref, h2_ref, gate_ref, tab_ref, ssum_ref, pa_ref, pb_ref, w_ref,
                   tile0_scr, tile1_scr, comb_scr):
    tm = h2_ref.shape[0]
    row_iota = lax.broadcasted_iota(jnp.int32, (TOK_GROUP, D_MODEL), 0)
    lane = lax.broadcasted_iota(jnp.int32, (TOK_GROUP, 2 * N_SEL), 1)
    tiles = (tile0_scr, tile1_scr)

    def group_body(g, carry):
        base = pl.multiple_of(g * TOK_GROUP, TOK_GROUP)
        xs = h2_ref[pl.ds(base, TOK_GROUP), :].astype(F32)
        acc = None
        for i in range(TOK_GROUP):
            tile = tiles[i % 2]
            _gather_rows(idx_ref, base + i, tab_ref, tile)
            w = jnp.concatenate([_tile_chunk(tile, j) for j in range(ROW_WORDS)], axis=-1)
            xm = jnp.where(row_iota == i, xs, 0.0)
            lhs = jnp.concatenate([xm[:, :D_MODEL // 2], xm[:, D_MODEL // 2:]], axis=0)
            d = _dot_nt(lhs.astype(BF16), w)
            acc = d if acc is None else acc + d
        comb_scr[pl.ds(base, TOK_GROUP), :] = jnp.where(lane % 2 == 0, acc[:TOK_GROUP], acc[TOK_GROUP:])
        return carry

    lax.fori_loop(0, tm // TOK_GROUP, group_body, 0)
    a = _dot_split(comb_scr[...], ssum_ref[...])
    act = 0.5 * a * (1.0 + lax.erf(a * (2.0 ** -0.5)))
    w = (gate_ref[...] * act).astype(BF16)
    w_ref[:, :2 * N_SEL] = _dot(w, pa_ref[...]).astype(BF16)
    w_ref[:, 2 * N_SEL:] = _dot(w, pb_ref[...]).astype(BF16)


def _peer_v_kernel(idx_ref, w_ref, x1_ref, mod_ref, fg_ref, tab_ref, o_ref, tile0_scr, tile1_scr, *, final):
    tm = w_ref.shape[0]
    row_iota = lax.broadcasted_iota(jnp.int32, (TOK_GROUP, 4 * N_SEL), 0)
    g2 = mod_ref[5:6, :]
    tiles = (tile0_scr, tile1_scr)

    def group_body(g, carry):
        base = pl.multiple_of(g * TOK_GROUP, TOK_GROUP)
        ws = w_ref[pl.ds(base, TOK_GROUP), :].astype(F32)
        accs = None
        for i in range(TOK_GROUP):
            tile = tiles[i % 2]
            _gather_rows(idx_ref, base + i, tab_ref, tile)
            wm = jnp.where(row_iota == i, ws, 0.0)
            lhs = jnp.concatenate([wm[:, :2 * N_SEL], wm[:, 2 * N_SEL:]], axis=0).astype(BF16)
            ds = [_dot(lhs, _tile_chunk(tile, j)) for j in range(ROW_WORDS)]
            accs = ds if accs is None else [a + d for a, d in zip(accs, ds)]
        y = jnp.concatenate([a[:TOK_GROUP] for a in accs] + [a[TOK_GROUP:] for a in accs], axis=-1)
        x2 = x1_ref[pl.ds(base, TOK_GROUP), :] + g2 * y
        if final:
            x2 = _rms(x2) * fg_ref[...]
        o_ref[pl.ds(base, TOK_GROUP), :] = x2
        return carry

    lax.fori_loop(0, tm // TOK_GROUP, group_body, 0)


def _table_spec(tab):
    return pl.BlockSpec(tab.shape, lambda i: (0, 0), pipeline_mode=pl.Buffered(1))


def _peer_u(eidx, h2, gate, utab, tabs, tm):
    T = h2.shape[0]
    return pl.pallas_call(
        _peer_u_kernel,
        grid=(T // tm,),
        in_specs=[pl.BlockSpec((tm, N_SEL), lambda i: (i, 0), memory_space=pltpu.SMEM),
                  pl.BlockSpec((tm, D_MODEL), lambda i: (i, 0)),
                  pl.BlockSpec((tm, N_SEL), lambda i: (i, 0)),
                  _table_spec(utab),
                  pl.BlockSpec((2 * N_SEL, N_SEL), lambda i: (0, 0)),
                  pl.BlockSpec((N_SEL, 2 * N_SEL), lambda i: (0, 0)),
                  pl.BlockSpec((N_SEL, 2 * N_SEL), lambda i: (0, 0))],
        out_specs=pl.BlockSpec((tm, 4 * N_SEL), lambda i: (i, 0)),
        out_shape=jax.ShapeDtypeStruct((T, 4 * N_SEL), BF16),
        scratch_shapes=[pltpu.VMEM((ROW_WORDS * GATHER_STRIDE, 128), jnp.uint32),
                        pltpu.VMEM((ROW_WORDS * GATHER_STRIDE, 128), jnp.uint32),
                        pltpu.VMEM((tm, 2 * N_SEL), F32)],
        compiler_params=_params(("arbitrary",)),
        name="peer_u",
    )(eidx, h2, gate, utab, tabs["ssum"], tabs["pa"], tabs["pb"])


def _peer_v(eidx, w, x1, mod_l, b_idx, final_g, vtab, tm, final):
    T = w.shape[0]
    tiles_per_batch = T // tm // b_idx[1]
    return pl.pallas_call(
        functools.partial(_peer_v_kernel, final=final),
        grid=(T // tm,),
        in_specs=[pl.BlockSpec((tm, N_SEL), lambda i: (i, 0), memory_space=pltpu.SMEM),
                  pl.BlockSpec((tm, 4 * N_SEL), lambda i: (i, 0)),
                  pl.BlockSpec((tm, D_MODEL), lambda i: (i, 0)),
                  pl.BlockSpec((None, 6, D_MODEL),
                               lambda i: (i // tiles_per_batch + b_idx[0], 0, 0)),
                  pl.BlockSpec((1, D_MODEL), lambda i: (0, 0)),
                  _table_spec(vtab)],
        out_specs=pl.BlockSpec((tm, D_MODEL), lambda i: (i, 0)),
        out_shape=jax.ShapeDtypeStruct((T, D_MODEL), F32),
        scratch_shapes=[pltpu.VMEM((ROW_WORDS * GATHER_STRIDE, 128), jnp.uint32),
                        pltpu.VMEM((ROW_WORDS * GATHER_STRIDE, 128), jnp.uint32)],
        compiler_params=_params(("arbitrary",)),
        name="peer_v",
    )(eidx, w, x1, mod_l, final_g, vtab)


def _pack_table(t):
    n = t.shape[0]
    b = lax.bitcast_convert_type(t.astype(BF16), jnp.uint16).astype(jnp.uint32)
    words = b[:, :D_MODEL // 2] | (b[:, D_MODEL // 2:] << 16)
    return words.reshape(n * ROW_WORDS, 128)


def _rope_lane_tables(S, dim):
    dq = dim // 4
    freqs = ROPE_THETA ** (-jnp.arange(dq, dtype=F32) / dq)
    t = jnp.arange(S, dtype=jnp.int32)
    row = (t // GRID_W).astype(F32)
    col = (t % GRID_W).astype(F32)
    ang = jnp.stack([row[:, None] * freqs, col[:, None] * freqs], axis=1)
    cos, sin = jnp.cos(ang), jnp.sin(ang)
    d = np.arange(dim)
    blk, f = d // (2 * dq), d % dq
    sign = np.where((d % (2 * dq)) < dq, -1.0, 1.0).astype(np.float32)
    return cos[:, blk, f], sin[:, blk, f] * sign


def _const_tables(S):
    cos_a, sin_a = _rope_lane_tables(S, HEAD_DIM)
    cos_c, sin_c = _rope_lane_tables(S, C_ROPE)
    one = jnp.ones((S, 128), F32)
    zero = jnp.zeros((S, 128), F32)
    cosk = one.at[:, :C_ROPE].set(cos_c)
    sink = zero.at[:, :C_ROPE].set(sin_c)
    cosh = one.at[:, C_NOPE:C_NOPE + C_ROPE].set(cos_c)
    sinh = zero.at[:, C_NOPE:C_NOPE + C_ROPE].set(sin_c)
    head_of = np.arange(A_Q) // HEAD_DIM
    epl = np.zeros((128, C_QW), np.float32)
    for h in range(C_HEADS):
        epl[np.arange(C_ROPE), h * 128 + C_NOPE + np.arange(C_ROPE)] = 1.0
    k = np.arange(N_SEL)
    ssum = np.zeros((2 * N_SEL, N_SEL), np.float32)
    ssum[2 * k, k] = 1.0
    ssum[2 * k + 1, k] = 1.0
    pa = np.zeros((N_SEL, 2 * N_SEL), np.float32)
    pb = np.zeros((N_SEL, 2 * N_SEL), np.float32)
    pa[k, 2 * k] = 1.0
    pb[k, 2 * k + 1] = 1.0
    return {
        "cosa": jnp.tile(cos_a, (1, A_HEADS)), "sina": jnp.tile(sin_a, (1, A_HEADS)),
        "cosc": jnp.tile(cosh, (1, C_HEADS)), "sinc": jnp.tile(sinh, (1, C_HEADS)),
        "cosk": cosk, "sink": sink,
        "m64": jnp.asarray(head_of[:, None] == head_of[None, :], BF16),
        "epl": jnp.asarray(epl, BF16), "ssum": jnp.asarray(ssum, BF16),
        "pa": jnp.asarray(pa, BF16), "pb": jnp.asarray(pb, BF16),
    }


def _layer_weights(l, norm1_g, norm2_g, w_in, a_q_norm, a_k_norm, c_q_norm, c_wqb, c_kv_norm, c_wkvb,
                   w_out, peer_wq, peer_subkeys, peer_u, peer_v):
    src = np.cumsum((0, A_Q, A_KV, A_KV, B_W, B_W, B_W, C_Q_RANK, C_KV_RANK))
    dst = (OFF_AQ, OFF_AK, OFF_AV, OFF_BQ, OFF_BK, OFF_BV, OFF_CQA, OFF_CKV, OFF_KPE)
    wid = (A_Q, A_KV, A_KV, B_W, B_W, B_W, C_Q_RANK, C_KV_RANK, C_ROPE)
    w1 = jnp.zeros((D_MODEL, IN_W), F32)
    for s, d, w in zip(src, dst, wid):
        w1 = w1.at[:, d:d + w].set(w_in[l][:, s:s + w])
    wqb = jnp.zeros((C_Q_RANK, C_HEADS, 128), F32).at[:, :, :C_NOPE + C_ROPE].set(
        c_wqb[l].reshape(C_Q_RANK, C_HEADS, C_NOPE + C_ROPE)).reshape(C_Q_RANK, C_QW)
    wkvb = c_wkvb[l].reshape(C_KV_RANK, C_HEADS, C_NOPE + C_V)
    wkn = jnp.zeros((C_KV_RANK, C_HEADS, 128), F32).at[:, :, :C_NOPE].set(
        wkvb[:, :, :C_NOPE]).reshape(C_KV_RANK, C_QW)
    wkv = wkvb[:, :, C_NOPE:].reshape(C_KV_RANK, C_VW)
    return {
        "g1": norm1_g[l][None, :], "g2": norm2_g[l][None, :],
        "w1": w1.astype(BF16),
        "aqn": jnp.tile(a_q_norm[l], A_HEADS)[None, :], "akn": jnp.tile(a_k_norm[l], A_KV_HEADS)[None, :],
        "cqn": c_q_norm[l][None, :], "ckvn": c_kv_norm[l][None, :],
        "wqb": wqb.astype(BF16), "wkn": wkn.astype(BF16), "wkv": wkv.astype(BF16),
        "wout": w_out[l].astype(BF16),
        "wqt": peer_wq[l].T.astype(BF16),
        "sk": peer_subkeys[l].reshape(2 * PEER_HEADS, N_KEYS, PEER_DKEY // 2).astype(BF16),
        "utab": _pack_table(peer_u[l]), "vtab": _pack_table(peer_v[l]),
    }


def _tile(n, pref):
    t = pref
    while n % t:
        t //= 2
    return t


def _trunk(x, mod, b_off, n_batch_total, layers, na_bias, final_g):
    B, S, _ = x.shape
    T = B * S
    tabs = _const_tables(S)
    tm = _tile(S, 256)
    tq = _tile(S, 256)
    tp = _tile(S, 256)
    depth = len(layers)
    for l, lw in enumerate(layers):
        aq, ak, av, bq, bk, bv, cq, ck, cv = _in_proj(x, mod[l], b_off, lw, tabs, tm)
        oa = _dense_attention(_attn_a_kernel, aq, ak, av, A_Q, tq, "attn_a")
        ob = _na_attention(bq, bk, bv, na_bias[l])
        oc = _dense_attention(_attn_c_kernel, cq, ck, cv, C_VW, tq, "attn_c")
        x1, h2 = _out_proj(oa, ob, oc, x, mod[l], b_off, lw, tm)
        h2 = h2.reshape(T, D_MODEL)
        eidx_t, gate_t = _peer_select(h2, lw, tp)
        eidx, gate = eidx_t.T, gate_t.T
        w = _peer_u(eidx, h2, gate, lw["utab"], tabs, tp)
        x = _peer_v(eidx, w, x1.reshape(T, D_MODEL), mod[l], (b_off, B), final_g[None, :],
                    lw["vtab"], tp, l == depth - 1).reshape(B, S, D_MODEL)
    return x


def kernel(x_prompt, x_sample, c_prompt, c_sample, ada_w, ada_b, norm1_g, norm2_g, w_in, a_q_norm, a_k_norm, b_rpb, c_q_norm, c_wqb, c_kv_norm, c_wkvb, w_out, peer_wq, peer_subkeys, peer_u, peer_v, final_g):
    depth = ada_w.shape[0]
    c_all = jnp.concatenate([c_prompt, c_sample], axis=0)
    mod = _modulation(c_all, ada_w, ada_b).reshape(depth, c_all.shape[0], 6, D_MODEL)
    layers = [_layer_weights(l, norm1_g, norm2_g, w_in, a_q_norm, a_k_norm, c_q_norm, c_wqb, c_kv_norm,
                             c_wkvb, w_out, peer_wq, peer_subkeys, peer_u, peer_v) for l in range(depth)]
    outs = []
    b_off = 0
    for x in (x_prompt, x_sample):
        rows = x.shape[1] // GRID_W
        na_bias = [_na_bias(b_rpb[l], rows) for l in range(depth)]
        outs.append(_trunk(x, mod, b_off, c_all.shape[0], layers, na_bias, final_g))
        b_off += x.shape[0]
    return tuple(outs)
```

```python
import functools

import numpy as np
import jax
import jax.numpy as jnp
from jax import lax
from jax.experimental import pallas as pl
from jax.experimental.pallas import tpu as pltpu

D_MODEL = 1024
GRID_W = 64
HEAD_DIM = 64
EPS = 1e-6
ROPE_THETA = 10000.0
A_HEADS = 6
A_KV_HEADS = 2
B_HEADS = 5
NA_ROWS = 8
NA_COLS = 16
C_HEADS = 5
C_NOPE = 64
C_ROPE = 32
C_V = 64
C_Q_RANK = 256
C_KV_RANK = 128
PEER_HEADS = 8
N_KEYS = 128
PEER_TOPK = 16
PEER_DKEY = 128
N_SEL = PEER_HEADS * PEER_TOPK

A_Q = A_HEADS * HEAD_DIM
A_KV = A_KV_HEADS * HEAD_DIM
B_W = B_HEADS * HEAD_DIM
C_QW = C_HEADS * 128
C_VW = C_HEADS * C_V

OFF_AQ, OFF_AK, OFF_AV = 0, 384, 512
OFF_BQ, OFF_BK, OFF_BV = 640, 1024, 1408
OFF_CQA, OFF_CKV, OFF_KPE = 1792, 2048, 2176
IN_W = 2304

NA_WIN_ROWS = NA_ROWS + 1
NA_KEYS = NA_WIN_ROWS * GRID_W
NEG_BIG = -1e30

V7X_VMEM_BYTES = 64 * 1024 * 1024
VMEM_LIMIT = 56 * 1024 * 1024

BF16 = jnp.bfloat16
F32 = jnp.float32


def _dot(a, b):
    return jnp.dot(a, b, preferred_element_type=F32)


def _dot_nt(a, b):
    return lax.dot_general(a, b, (((1,), (1,)), ((), ())), preferred_element_type=F32)


def _dot_split(a, m):
    hi = a.astype(BF16)
    lo = (a - hi.astype(F32)).astype(BF16)
    return _dot(hi, m) + _dot(lo, m)


def _rms(x):
    return x * lax.rsqrt(jnp.mean(x * x, axis=-1, keepdims=True) + EPS)


def _rope(x, cos, sin_signed, half):
    n = x.shape[1]
    fwd = pltpu.roll(x, n - half, 1)
    bwd = pltpu.roll(x, half, 1)
    lane = lax.broadcasted_iota(jnp.int32, x.shape, 1)
    partner = jnp.where((lane % (2 * half)) < half, fwd, bwd)
    return x * cos + partner * sin_signed


def _params(sem):
    return pltpu.CompilerParams(dimension_semantics=sem, vmem_limit_bytes=VMEM_LIMIT)


def _mod_kernel(c_ref, w_ref, b_ref, o_ref):
    c = c_ref[...]
    sc = (c * jax.nn.sigmoid(c)).astype(BF16)
    o_ref[0] = _dot(sc, w_ref[0].astype(BF16)) + b_ref[0]


def _modulation(c_all, ada_w, ada_b):
    depth = ada_w.shape[0]
    nb = c_all.shape[0]
    tn = 1536
    return pl.pallas_call(
        _mod_kernel,
        grid=(depth, 6 * D_MODEL // tn),
        in_specs=[
            pl.BlockSpec((nb, D_MODEL), lambda l, j: (0, 0)),
            pl.BlockSpec((1, D_MODEL, tn), lambda l, j: (l, 0, j)),
            pl.BlockSpec((1, 1, tn), lambda l, j: (l, 0, j)),
        ],
        out_specs=pl.BlockSpec((1, nb, tn), lambda l, j: (l, 0, j)),
        out_shape=jax.ShapeDtypeStruct((depth, nb, 6 * D_MODEL), F32),
        compiler_params=_params(("arbitrary", "arbitrary")),
        name="modulation",
    )(c_all, ada_w, ada_b.reshape(depth, 1, 6 * D_MODEL))


def _in_proj_kernel(x_ref, mod_ref, g1_ref, w1_ref, aqn_ref, akn_ref, cqn_ref, ckvn_ref,
                    wqb_ref, wkn_ref, wkv_ref, epl_ref, m64_ref,
                    cosa_ref, sina_ref, cosc_ref, sinc_ref, cosk_ref, sink_ref,
                    aq_ref, ak_ref, av_ref, bq_ref, bk_ref, bv_ref, cq_ref, ck_ref, cv_ref):
    x = x_ref[0]
    shift, scale = mod_ref[0:1, :], mod_ref[1:2, :]
    h = _rms(x) * g1_ref[...]
    h = h * (1.0 + scale) + shift
    z = _dot(h.astype(BF16), w1_ref[...])

    m64 = m64_ref[...]
    zq = z[:, OFF_AQ:OFF_AQ + A_Q]
    msq = _dot_split(zq * zq, m64) * (1.0 / HEAD_DIM)
    qn = zq * lax.rsqrt(msq + EPS) * aqn_ref[...]
    qn = _rope(qn, cosa_ref[...], sina_ref[...], HEAD_DIM // 4)
    aq_ref[0] = (qn * (HEAD_DIM ** -0.5)).astype(BF16)

    zk = z[:, OFF_AK:OFF_AK + A_KV]
    msk = _dot_split(zk * zk, m64[:A_KV, :A_KV]) * (1.0 / HEAD_DIM)
    kn = zk * lax.rsqrt(msk + EPS) * akn_ref[...]
    kn = _rope(kn, cosa_ref[:, :A_KV], sina_ref[:, :A_KV], HEAD_DIM // 4)
    ak_ref[0] = kn.astype(BF16)
    av_ref[0] = z[:, OFF_AV:OFF_AV + A_KV].astype(BF16)

    bq_ref[0] = (z[:, OFF_BQ:OFF_BQ + B_W] * (HEAD_DIM ** -0.5)).astype(BF16)
    bk_ref[0] = z[:, OFF_BK:OFF_BK + B_W].astype(BF16)
    bv_ref[0] = z[:, OFF_BV:OFF_BV + B_W].astype(BF16)

    cqa = _rms(z[:, OFF_CQA:OFF_CQA + C_Q_RANK]) * cqn_ref[...]
    cq = _dot(cqa.astype(BF16), wqb_ref[...])
    cq = _rope(cq, cosc_ref[...], sinc_ref[...], C_ROPE // 4)
    cq_ref[0] = (cq * ((C_NOPE + C_ROPE) ** -0.5)).astype(BF16)

    ckv = (_rms(z[:, OFF_CKV:OFF_CKV + C_KV_RANK]) * ckvn_ref[...]).astype(BF16)
    kpe = _rope(z[:, OFF_KPE:OFF_KPE + 128], cosk_ref[...], sink_ref[...], C_ROPE // 4)
    ck = _dot(ckv, wkn_ref[...]) + _dot(kpe.astype(BF16), epl_ref[...])
    ck_ref[0] = ck.astype(BF16)
    cv_ref[0] = _dot(ckv, wkv_ref[...]).astype(BF16)


def _in_proj(x, mod_l, b_off, lw, tabs, tm):
    B, S, _ = x.shape
    nt = S // tm
    tok = lambda w: pl.BlockSpec((1, tm, w), lambda b, i: (b, i, 0))
    full = lambda a: pl.BlockSpec(a.shape, lambda b, i: (0,) * a.ndim)
    tab = lambda w: pl.BlockSpec((tm, w), lambda b, i: (i, 0))
    widths = (A_Q, A_KV, A_KV, B_W, B_W, B_W, C_QW, C_QW, C_VW)
    consts = (lw["g1"], lw["w1"], lw["aqn"], lw["akn"], lw["cqn"], lw["ckvn"],
              lw["wqb"], lw["wkn"], lw["wkv"], tabs["epl"], tabs["m64"])
    return pl.pallas_call(
        _in_proj_kernel,
        grid=(B, nt),
        in_specs=[tok(D_MODEL),
                  pl.BlockSpec((None, 6, D_MODEL), lambda b, i: (b + b_off, 0, 0))]
                 + [full(a) for a in consts]
                 + [tab(A_Q), tab(A_Q), tab(C_QW), tab(C_QW), tab(128), tab(128)],
        out_specs=[tok(w) for w in widths],
        out_shape=[jax.ShapeDtypeStruct((B, S, w), BF16) for w in widths],
        compiler_params=_params(("parallel", "parallel")),
        name="in_proj",
    )(x, mod_l, *consts, tabs["cosa"], tabs["sina"], tabs["cosc"], tabs["sinc"],
      tabs["cosk"], tabs["sink"])


def _softmax_pv(s, v):
    m = jnp.max(s, axis=-1, keepdims=True)
    p = jnp.exp(s - m)
    l = jnp.sum(p, axis=-1, keepdims=True)
    return _dot(p.astype(BF16), v) / l


def _attn_a_kernel(q_ref, k_ref, v_ref, o_ref):
    group = A_HEADS // A_KV_HEADS
    outs = []
    for g in range(A_KV_HEADS):
        k = k_ref[0, :, g * HEAD_DIM:(g + 1) * HEAD_DIM]
        v = v_ref[0, :, g * HEAD_DIM:(g + 1) * HEAD_DIM]
        for j in range(group):
            h = g * group + j
            q = q_ref[0, :, h * HEAD_DIM:(h + 1) * HEAD_DIM]
            outs.append(_softmax_pv(_dot_nt(q, k), v))
    o_ref[0] = jnp.concatenate(outs, axis=-1).astype(BF16)


def _attn_c_kernel(q_ref, k_ref, v_ref, o_ref):
    outs = []
    for h in range(C_HEADS):
        q = q_ref[0, :, h * 128:(h + 1) * 128]
        k = k_ref[0, :, h * 128:(h + 1) * 128]
        v = v_ref[0, :, h * C_V:(h + 1) * C_V]
        outs.append(_softmax_pv(_dot_nt(q, k), v))
    o_ref[0] = jnp.concatenate(outs, axis=-1).astype(BF16)


def _dense_attention(kernel, q, k, v, out_w, tq, name):
    B, S, _ = q.shape
    return pl.pallas_call(
        kernel,
        grid=(B, S // tq),
        in_specs=[pl.BlockSpec((1, tq, q.shape[2]), lambda b, i: (b, i, 0)),
                  pl.BlockSpec((1, S, k.shape[2]), lambda b, i: (b, 0, 0)),
                  pl.BlockSpec((1, S, v.shape[2]), lambda b, i: (b, 0, 0))],
        out_specs=pl.BlockSpec((1, tq, out_w), lambda b, i: (b, i, 0)),
        out_shape=jax.ShapeDtypeStruct((B, S, out_w), BF16),
        compiler_params=_params(("parallel", "arbitrary")),
        name=name,
    )(q, k, v)


def _na_case(i, nblk):
    return jnp.where(i == 0, 0, jnp.where(i == 1, 1, jnp.where(i == nblk - 2, 3,
                     jnp.where(i == nblk - 1, 4, 2))))


def _attn_b_kernel(q_ref, k_ref, v_ref, bias_ref, o_ref, *, rows):
    i = pl.program_id(1)
    wstart = jnp.clip(2 * i - NA_ROWS // 2, 0, rows - NA_WIN_ROWS)
    start = pl.multiple_of(wstart * GRID_W, GRID_W)
    kw = k_ref[0, pl.ds(start, NA_KEYS), :]
    vw = v_ref[0, pl.ds(start, NA_KEYS), :]
    outs = []
    for h in range(B_HEADS):
        sl = slice(h * HEAD_DIM, (h + 1) * HEAD_DIM)
        s = _dot_nt(q_ref[0, :, sl], kw[:, sl]) + bias_ref[h]
        outs.append(_softmax_pv(s, vw[:, sl]))
    o_ref[0] = jnp.concatenate(outs, axis=-1).astype(BF16)


def _na_attention(q, k, v, bias):
    B, S, _ = q.shape
    rows = S // GRID_W
    nblk = rows // 2
    tq = 2 * GRID_W
    return pl.pallas_call(
        functools.partial(_attn_b_kernel, rows=rows),
        grid=(B, nblk),
        in_specs=[pl.BlockSpec((1, tq, B_W), lambda b, i: (b, i, 0)),
                  pl.BlockSpec((1, S, B_W), lambda b, i: (b, 0, 0)),
                  pl.BlockSpec((1, S, B_W), lambda b, i: (b, 0, 0)),
                  pl.BlockSpec((None, B_HEADS, tq, NA_KEYS),
                               lambda b, i: (_na_case(i, nblk), 0, 0, 0))],
        out_specs=pl.BlockSpec((1, tq, B_W), lambda b, i: (b, i, 0)),
        out_shape=jax.ShapeDtypeStruct((B, S, B_W), BF16),
        compiler_params=_params(("parallel", "arbitrary")),
        name="attn_b",
    )(q, k, v, bias)


def _na_bias(rpb, rows):
    assert rows >= NA_ROWS + 4 and rows % 2 == 0
    nblk = rows // 2
    wr = min(NA_ROWS, rows)
    qc = np.arange(GRID_W)
    cs = np.clip(qc - NA_COLS // 2, 0, GRID_W - NA_COLS)
    in_c = (qc[None, :] >= cs[:, None]) & (qc[None, :] < cs[:, None] + NA_COLS)
    cidx = qc[None, :] - qc[:, None] + NA_COLS - 1
    toeplitz = (cidx[None] == np.arange(2 * NA_COLS - 1)[:, None, None]) & in_c[None]
    ridx = np.zeros((5, 2, NA_WIN_ROWS), np.int64)
    in_r = np.zeros((5, 2, NA_WIN_ROWS), bool)
    for case, blk in enumerate((0, 1, 2, nblk - 2, nblk - 1)):
        wstart = int(np.clip(2 * blk - NA_ROWS // 2, 0, rows - NA_WIN_ROWS))
        for j in range(2):
            qr = 2 * blk + j
            rs = int(np.clip(qr - wr // 2, 0, rows - wr))
            kr = wstart + np.arange(NA_WIN_ROWS)
            in_r[case, j] = (kr >= rs) & (kr < rs + wr)
            ridx[case, j] = np.clip(kr - qr + NA_ROWS - 1, 0, 2 * NA_ROWS - 2)
    picked = rpb[:, ridx, :]
    bias = jnp.einsum("hcjkd,dxy->chjxky", picked, jnp.asarray(toeplitz, F32),
                      precision=lax.Precision.HIGHEST)
    mask = in_r[:, None, :, None, :, None] & in_c[None, None, None, :, None, :]
    bias = jnp.where(mask, bias, NEG_BIG)
    return bias.reshape(5, rpb.shape[0], 2 * GRID_W, NA_KEYS)


def _out_proj_kernel(oa_ref, ob_ref, oc_ref, x_ref, mod_ref, g2_ref, w_ref, x1_ref, h2_ref):
    y = (_dot(oa_ref[0], w_ref[0:A_Q, :])
         + _dot(ob_ref[0], w_ref[A_Q:A_Q + B_W, :])
         + _dot(oc_ref[0], w_ref[A_Q + B_W:, :]))
    x1 = x_ref[0] + mod_ref[2:3, :] * y
    x1_ref[0] = x1
    h2 = _rms(x1) * g2_ref[...]
    h2_ref[0] = (h2 * (1.0 + mod_ref[4:5, :]) + mod_ref[3:4, :]).astype(BF16)


def _out_proj(oa, ob, oc, x, mod_l, b_off, lw, tm):
    B, S, _ = x.shape
    tok = lambda w: pl.BlockSpec((1, tm, w), lambda b, i: (b, i, 0))
    return pl.pallas_call(
        _out_proj_kernel,
        grid=(B, S // tm),
        in_specs=[tok(A_Q), tok(B_W), tok(C_VW), tok(D_MODEL),
                  pl.BlockSpec((None, 6, D_MODEL), lambda b, i: (b + b_off, 0, 0)),
                  pl.BlockSpec((1, D_MODEL), lambda b, i: (0, 0)),
                  pl.BlockSpec((D_MODEL, D_MODEL), lambda b, i: (0, 0))],
        out_specs=[tok(D_MODEL), tok(D_MODEL)],
        out_shape=[jax.ShapeDtypeStruct((B, S, D_MODEL), F32),
                   jax.ShapeDtypeStruct((B, S, D_MODEL), BF16)],
        compiler_params=_params(("parallel", "parallel")),
        name="out_proj",
    )(oa, ob, oc, x, mod_l, lw["g2"], lw["wout"])


def _extract_max(s, order, n):
    m = jnp.max(s, axis=0, keepdims=True)
    am = jnp.min(jnp.where(s == m, order, n), axis=0, keepdims=True)
    return m, am, jnp.where(order == am, -jnp.inf, s)


def _drain(steps):
    for _ in steps:
        pass


def _alternate(*streams):
    streams = list(streams)
    while streams:
        for entry in list(streams):
            g, per_turn = entry
            for _ in range(per_turn):
                if next(g, StopIteration) is StopIteration:
                    streams.remove(entry)
                    break


def _select_head_fn(qt_scr, sk_ref, eidx_ref, gate_ref, tm):
    lanes = 128
    iota_k = lax.broadcasted_iota(jnp.int32, (N_KEYS, lanes), 0)
    half = PEER_TOPK // 2
    dk = PEER_DKEY // 2
    iota8 = lax.broadcasted_iota(jnp.int32, (half, lanes), 0)
    flat = jnp.concatenate([iota8 * PEER_TOPK + b for b in range(half)]
                           + [iota8 + half, (iota8 + half) * PEER_TOPK], axis=0)
    n_flat = PEER_TOPK * PEER_TOPK

    def scores(hp, lo):
        q = qt_scr[pl.ds(pl.multiple_of(hp * dk, dk), dk), lo:lo + lanes]
        return _dot(sk_ref[hp], q)

    def head(h):
        for lo in range(0, tm, lanes):
            yield from head_pass(h, lo)

    def head_pass(h, lo):
        s = [scores(2 * h, lo), scores(2 * h + 1, lo)]
        vals, idxs = ([], []), ([], [])
        yield
        for _ in range(PEER_TOPK):
            for p in range(2):
                m, am, s[p] = _extract_max(s[p], iota_k, N_KEYS)
                vals[p].append(m)
                idxs[p].append(am)
            yield
        v0, v1 = (jnp.concatenate(v, axis=0) for v in vals)
        i0, i1 = (jnp.concatenate(i, axis=0) for i in idxs)
        cand, cidx = [], []
        for b in range(half):
            c = v0[:half] + v1[b:b + 1]
            a_max = PEER_TOPK // (b + 1) - 1
            cand.append(c if a_max >= half - 1 else jnp.where(iota8 <= a_max, c, -jnp.inf))
            cidx.append(i0[:half] * N_KEYS + i1[b:b + 1])
        cand += [v0[0:1] + v1[half:], v0[half:] + v1[0:1]]
        cidx += [i0[0:1] * N_KEYS + i1[half:], i0[half:] * N_KEYS + i1[0:1]]
        cand = jnp.concatenate(cand, axis=0)
        cidx = jnp.concatenate(cidx, axis=0)
        yield
        tv, te = [], []
        for _ in range(PEER_TOPK):
            m, am, cand = _extract_max(cand, flat, n_flat)
            te.append(jnp.sum(jnp.where(flat == am, cidx, 0), axis=0, keepdims=True))
            tv.append(m)
            yield
        tv = jnp.concatenate(tv, axis=0)
        ex = jnp.exp(tv - tv[0:1])
        row = pl.multiple_of(h * PEER_TOPK, PEER_TOPK)
        gate_ref[pl.ds(row, PEER_TOPK), lo:lo + lanes] = ex / jnp.sum(ex, axis=0, keepdims=True)
        eidx_ref[pl.ds(row, PEER_TOPK), lo:lo + lanes] = jnp.concatenate(te, axis=0) * 4

    return head


def _peer_select_kernel(h2_ref, wqt_ref, sk_ref, eidx_ref, gate_ref, qt_scr):
    tm = h2_ref.shape[0]
    qt_scr[...] = _dot_nt(wqt_ref[...], h2_ref[...]).astype(BF16)
    head = _select_head_fn(qt_scr, sk_ref, eidx_ref, gate_ref, tm)

    def head_body(h, carry):
        _drain(head(h))
        return carry

    lax.fori_loop(0, PEER_HEADS, head_body, 0)


def _select_specs(tm, tile_off):
    in_specs = [pl.BlockSpec((tm, D_MODEL), lambda i: (i + tile_off, 0)),
                pl.BlockSpec((D_MODEL, D_MODEL), lambda i: (0, 0)),
                pl.BlockSpec((2 * PEER_HEADS, N_KEYS, PEER_DKEY // 2), lambda i: (0, 0, 0))]
    out_specs = [pl.BlockSpec((N_SEL, tm), lambda i: (0, i)), pl.BlockSpec((N_SEL, tm), lambda i: (0, i))]
    out_shape = lambda n: [jax.ShapeDtypeStruct((N_SEL, n), jnp.int32), jax.ShapeDtypeStruct((N_SEL, n), F32)]
    return in_specs, out_specs, out_shape, pltpu.VMEM((D_MODEL, tm), BF16)


def _peer_select(h2, lw, tm, tile_off, n_tok):
    in_specs, out_specs, out_shape, qt = _select_specs(tm, tile_off)
    return pl.pallas_call(
        _peer_select_kernel,
        grid=(n_tok // tm,),
        in_specs=in_specs,
        out_specs=out_specs,
        out_shape=out_shape(n_tok),
        scratch_shapes=[qt],
        compiler_params=_params(("parallel",)),
        name="peer_select",
    )(h2, lw["wqt"], lw["sk"])


ROW_WORDS = D_MODEL // 2 // 128
GATHER_STRIDE = N_SEL + 1
TOK_GROUP = 16
PEER_CHUNKS = 8


def _gather_rows(idx_ref, t, tab_ref, tile_ref):
    for k in range(N_SEL):
        e4 = pl.multiple_of(idx_ref[t, k], ROW_WORDS)
        tile_ref[pl.ds(k, ROW_WORDS, stride=GATHER_STRIDE), :] = tab_ref[pl.ds(e4, ROW_WORDS), :]


def _tile_chunk(tile_ref, j):
    return pltpu.bitcast(tile_ref[pl.ds(j * GATHER_STRIDE, N_SEL), :], BF16)


def _u_group_fn(idx_ref, h2_ref, tab_ref, tiles, comb_scr):
    row_iota = lax.broadcasted_iota(jnp.int32, (TOK_GROUP, D_MODEL), 0)
    lane = lax.broadcasted_iota(jnp.int32, (TOK_GROUP, 2 * N_SEL), 1)

    def group(base):
        xs = h2_ref[pl.ds(base, TOK_GROUP), :].astype(F32)
        acc = None
        for i in range(TOK_GROUP):
            tile = tiles[i % 2]
            _gather_rows(idx_ref, base + i, tab_ref, tile)
            w = jnp.concatenate([_tile_chunk(tile, j) for j in range(ROW_WORDS)], axis=-1)
            xm = jnp.where(row_iota == i, xs, 0.0)
            lhs = jnp.concatenate([xm[:, :D_MODEL // 2], xm[:, D_MODEL // 2:]], axis=0)
            d = _dot_nt(lhs.astype(BF16), w)
            acc = d if acc is None else acc + d
            yield
        comb_scr[pl.ds(base, TOK_GROUP), :] = jnp.where(lane % 2 == 0, acc[:TOK_GROUP], acc[TOK_GROUP:])

    return group


def _u_finish(comb_scr, gate_ref, ssum_ref, pa_ref, pb_ref, w_ref):
    a = _dot_split(comb_scr[...], ssum_ref[...])
    act = 0.5 * a * (1.0 + lax.erf(a * (2.0 ** -0.5)))
    w = (gate_ref[...] * act).astype(BF16)
    w_ref[:, :2 * N_SEL] = _dot(w, pa_ref[...]).astype(BF16)
    w_ref[:, 2 * N_SEL:] = _dot(w, pb_ref[...]).astype(BF16)


def _peer_u_kernel(idx_ref, h2_ref, gate_ref, tab_ref, ssum_ref, pa_ref, pb_ref, w_ref,
                   tile0_scr, tile1_scr, comb_scr):
    group = _u_group_fn(idx_ref, h2_ref, tab_ref, (tile0_scr, tile1_scr), comb_scr)

    def group_body(g, carry):
        _drain(group(pl.multiple_of(g * TOK_GROUP, TOK_GROUP)))
        return carry

    lax.fori_loop(0, h2_ref.shape[0] // TOK_GROUP, group_body, 0)
    _u_finish(comb_scr, gate_ref, ssum_ref, pa_ref, pb_ref, w_ref)


def _peer_u_select_kernel(idx_ref, h2_ref, gate_ref, tab_ref, ssum_ref, pa_ref, pb_ref,
                          h2n_ref, wqt_ref, sk_ref, w_ref, eidx_ref, gaten_ref,
                          tile0_scr, tile1_scr, comb_scr, qt_scr):
    tm = h2_ref.shape[0]
    per_head = tm // PEER_HEADS
    qt_scr[...] = _dot_nt(wqt_ref[...], h2n_ref[...]).astype(BF16)
    group = _u_group_fn(idx_ref, h2_ref, tab_ref, (tile0_scr, tile1_scr), comb_scr)
    head = _select_head_fn(qt_scr, sk_ref, eidx_ref, gaten_ref, tm)

    def gather_steps(h):
        for g in range(per_head // TOK_GROUP):
            yield from group(pl.multiple_of(h * per_head + g * TOK_GROUP, TOK_GROUP))

    def body(h, carry):
        _alternate((gather_steps(h), 1), (head(h), tm // 128))
        return carry

    lax.fori_loop(0, PEER_HEADS, body, 0)
    _u_finish(comb_scr, gate_ref, ssum_ref, pa_ref, pb_ref, w_ref)


def _peer_v_kernel(idx_ref, w_ref, x1_ref, mod_ref, fg_ref, tab_ref, o_ref, tile0_scr, tile1_scr, *, final):
    tm = w_ref.shape[0]
    row_iota = lax.broadcasted_iota(jnp.int32, (TOK_GROUP, 4 * N_SEL), 0)
    g2 = mod_ref[5:6, :]
    tiles = (tile0_scr, tile1_scr)

    def group_body(g, carry):
        base = pl.multiple_of(g * TOK_GROUP, TOK_GROUP)
        ws = w_ref[pl.ds(base, TOK_GROUP), :].astype(F32)
        accs = None
        for i in range(TOK_GROUP):
            tile = tiles[i % 2]
            _gather_rows(idx_ref, base + i, tab_ref, tile)
            wm = jnp.where(row_iota == i, ws, 0.0)
            lhs = jnp.concatenate([wm[:, :2 * N_SEL], wm[:, 2 * N_SEL:]], axis=0).astype(BF16)
            ds = [_dot(lhs, _tile_chunk(tile, j)) for j in range(ROW_WORDS)]
            accs = ds if accs is None else [a + d for a, d in zip(accs, ds)]
        y = jnp.concatenate([a[:TOK_GROUP] for a in accs] + [a[TOK_GROUP:] for a in accs], axis=-1)
        x2 = x1_ref[pl.ds(base, TOK_GROUP), :] + g2 * y
        if final:
            x2 = _rms(x2) * fg_ref[...]
        o_ref[pl.ds(base, TOK_GROUP), :] = x2
        return carry

    lax.fori_loop(0, tm // TOK_GROUP, group_body, 0)


def _table_spec(tab):
    return pl.BlockSpec(tab.shape, lambda i: (0, 0), pipeline_mode=pl.Buffered(1))


def _u_specs(tm, tile_off, utab):
    in_specs = [pl.BlockSpec((tm, N_SEL), lambda i: (i, 0), memory_space=pltpu.SMEM),
                pl.BlockSpec((tm, D_MODEL), lambda i: (i + tile_off, 0)),
                pl.BlockSpec((tm, N_SEL), lambda i: (i, 0)),
                _table_spec(utab),
                pl.BlockSpec((2 * N_SEL, N_SEL), lambda i: (0, 0)),
                pl.BlockSpec((N_SEL, 2 * N_SEL), lambda i: (0, 0)),
                pl.BlockSpec((N_SEL, 2 * N_SEL), lambda i: (0, 0))]
    scratch = [pltpu.VMEM((ROW_WORDS * GATHER_STRIDE, 128), jnp.uint32),
               pltpu.VMEM((ROW_WORDS * GATHER_STRIDE, 128), jnp.uint32),
               pltpu.VMEM((tm, 2 * N_SEL), F32)]
    return in_specs, pl.BlockSpec((tm, 4 * N_SEL), lambda i: (i, 0)), scratch


def _peer_u(eidx, h2, gate, utab, tabs, tm, tile_off):
    n_tok = eidx.shape[0]
    in_specs, out_spec, scratch = _u_specs(tm, tile_off, utab)
    return pl.pallas_call(
        _peer_u_kernel,
        grid=(n_tok // tm,),
        in_specs=in_specs,
        out_specs=out_spec,
        out_shape=jax.ShapeDtypeStruct((n_tok, 4 * N_SEL), BF16),
        scratch_shapes=scratch,
        compiler_params=_params(("arbitrary",)),
        name="peer_u",
    )(eidx, h2, gate, utab, tabs["ssum"], tabs["pa"], tabs["pb"])


def _peer_u_select(eidx, h2, gate, utab, tabs, lw, tm, tile_off):
    n_tok = eidx.shape[0]
    in_specs, out_spec, scratch = _u_specs(tm, tile_off, utab)
    sel_in, sel_out, sel_shape, qt = _select_specs(tm, tile_off + n_tok // tm)
    return pl.pallas_call(
        _peer_u_select_kernel,
        grid=(n_tok // tm,),
        in_specs=in_specs + sel_in,
        out_specs=[out_spec] + sel_out,
        out_shape=[jax.ShapeDtypeStruct((n_tok, 4 * N_SEL), BF16)] + sel_shape(n_tok),
        scratch_shapes=scratch + [qt],
        compiler_params=_params(("arbitrary",)),
        name="peer_u_select",
    )(eidx, h2, gate, utab, tabs["ssum"], tabs["pa"], tabs["pb"], h2, lw["wqt"], lw["sk"])


def _peer_v(eidx, w, x1, mod_l, b_idx, final_g, vtab, tm, final):
    T = w.shape[0]
    tiles_per_batch = T // tm // b_idx[1]
    return pl.pallas_call(
        functools.partial(_peer_v_kernel, final=final),
        grid=(T // tm,),
        in_specs=[pl.BlockSpec((tm, N_SEL), lambda i: (i, 0), memory_space=pltpu.SMEM),
                  pl.BlockSpec((tm, 4 * N_SEL), lambda i: (i, 0)),
                  pl.BlockSpec((tm, D_MODEL), lambda i: (i, 0)),
                  pl.BlockSpec((None, 6, D_MODEL),
                               lambda i: (i // tiles_per_batch + b_idx[0], 0, 0)),
                  pl.BlockSpec((1, D_MODEL), lambda i: (0, 0)),
                  _table_spec(vtab)],
        out_specs=pl.BlockSpec((tm, D_MODEL), lambda i: (i, 0)),
        out_shape=jax.ShapeDtypeStruct((T, D_MODEL), F32),
        scratch_shapes=[pltpu.VMEM((ROW_WORDS * GATHER_STRIDE, 128), jnp.uint32),
                        pltpu.VMEM((ROW_WORDS * GATHER_STRIDE, 128), jnp.uint32)],
        compiler_params=_params(("arbitrary",)),
        name="peer_v",
    )(eidx, w, x1, mod_l, final_g, vtab)


def _pack_table(t):
    n = t.shape[0]
    b = lax.bitcast_convert_type(t.astype(BF16), jnp.uint16).astype(jnp.uint32)
    words = b[:, :D_MODEL // 2] | (b[:, D_MODEL // 2:] << 16)
    return words.reshape(n * ROW_WORDS, 128)


def _rope_lane_tables(S, dim):
    dq = dim // 4
    freqs = ROPE_THETA ** (-jnp.arange(dq, dtype=F32) / dq)
    t = jnp.arange(S, dtype=jnp.int32)
    row = (t // GRID_W).astype(F32)
    col = (t % GRID_W).astype(F32)
    ang = jnp.stack([row[:, None] * freqs, col[:, None] * freqs], axis=1)
    cos, sin = jnp.cos(ang), jnp.sin(ang)
    d = np.arange(dim)
    blk, f = d // (2 * dq), d % dq
    sign = np.where((d % (2 * dq)) < dq, -1.0, 1.0).astype(np.float32)
    return cos[:, blk, f], sin[:, blk, f] * sign


def _const_tables(S):
    cos_a, sin_a = _rope_lane_tables(S, HEAD_DIM)
    cos_c, sin_c = _rope_lane_tables(S, C_ROPE)
    one = jnp.ones((S, 128), F32)
    zero = jnp.zeros((S, 128), F32)
    cosk = one.at[:, :C_ROPE].set(cos_c)
    sink = zero.at[:, :C_ROPE].set(sin_c)
    cosh = one.at[:, C_NOPE:C_NOPE + C_ROPE].set(cos_c)
    sinh = zero.at[:, C_NOPE:C_NOPE + C_ROPE].set(sin_c)
    head_of = np.arange(A_Q) // HEAD_DIM
    epl = np.zeros((128, C_QW), np.float32)
    for h in range(C_HEADS):
        epl[np.arange(C_ROPE), h * 128 + C_NOPE + np.arange(C_ROPE)] = 1.0
    k = np.arange(N_SEL)
    ssum = np.zeros((2 * N_SEL, N_SEL), np.float32)
    ssum[2 * k, k] = 1.0
    ssum[2 * k + 1, k] = 1.0
    pa = np.zeros((N_SEL, 2 * N_SEL), np.float32)
    pb = np.zeros((N_SEL, 2 * N_SEL), np.float32)
    pa[k, 2 * k] = 1.0
    pb[k, 2 * k + 1] = 1.0
    return {
        "cosa": jnp.tile(cos_a, (1, A_HEADS)), "sina": jnp.tile(sin_a, (1, A_HEADS)),
        "cosc": jnp.tile(cosh, (1, C_HEADS)), "sinc": jnp.tile(sinh, (1, C_HEADS)),
        "cosk": cosk, "sink": sink,
        "m64": jnp.asarray(head_of[:, None] == head_of[None, :], BF16),
        "epl": jnp.asarray(epl, BF16), "ssum": jnp.asarray(ssum, BF16),
        "pa": jnp.asarray(pa, BF16), "pb": jnp.asarray(pb, BF16),
    }


def _layer_weights(l, norm1_g, norm2_g, w_in, a_q_norm, a_k_norm, c_q_norm, c_wqb, c_kv_norm, c_wkvb,
                   w_out, peer_wq, peer_subkeys, peer_u, peer_v):
    src = np.cumsum((0, A_Q, A_KV, A_KV, B_W, B_W, B_W, C_Q_RANK, C_KV_RANK))
    dst = (OFF_AQ, OFF_AK, OFF_AV, OFF_BQ, OFF_BK, OFF_BV, OFF_CQA, OFF_CKV, OFF_KPE)
    wid = (A_Q, A_KV, A_KV, B_W, B_W, B_W, C_Q_RANK, C_KV_RANK, C_ROPE)
    w1 = jnp.zeros((D_MODEL, IN_W), F32)
    for s, d, w in zip(src, dst, wid):
        w1 = w1.at[:, d:d + w].set(w_in[l][:, s:s + w])
    wqb = jnp.zeros((C_Q_RANK, C_HEADS, 128), F32).at[:, :, :C_NOPE + C_ROPE].set(
        c_wqb[l].reshape(C_Q_RANK, C_HEADS, C_NOPE + C_ROPE)).reshape(C_Q_RANK, C_QW)
    wkvb = c_wkvb[l].reshape(C_KV_RANK, C_HEADS, C_NOPE + C_V)
    wkn = jnp.zeros((C_KV_RANK, C_HEADS, 128), F32).at[:, :, :C_NOPE].set(
        wkvb[:, :, :C_NOPE]).reshape(C_KV_RANK, C_QW)
    wkv = wkvb[:, :, C_NOPE:].reshape(C_KV_RANK, C_VW)
    return {
        "g1": norm1_g[l][None, :], "g2": norm2_g[l][None, :],
        "w1": w1.astype(BF16),
        "aqn": jnp.tile(a_q_norm[l], A_HEADS)[None, :], "akn": jnp.tile(a_k_norm[l], A_KV_HEADS)[None, :],
        "cqn": c_q_norm[l][None, :], "ckvn": c_kv_norm[l][None, :],
        "wqb": wqb.astype(BF16), "wkn": wkn.astype(BF16), "wkv": wkv.astype(BF16),
        "wout": w_out[l].astype(BF16),
        "wqt": peer_wq[l].T.astype(BF16),
        "sk": peer_subkeys[l].reshape(2 * PEER_HEADS, N_KEYS, PEER_DKEY // 2).astype(BF16),
        "utab": _pack_table(peer_u[l]), "vtab": _pack_table(peer_v[l]),
    }


def _tile(n, pref):
    t = pref
    while n % t:
        t //= 2
    return t


def _trunk(x, mod, b_off, n_batch_total, layers, na_bias, final_g):
    B, S, _ = x.shape
    T = B * S
    tabs = _const_tables(S)
    tm = _tile(S, 256)
    tq = _tile(S, 256)
    tp = _tile(S, 256)
    depth = len(layers)
    for l, lw in enumerate(layers):
        aq, ak, av, bq, bk, bv, cq, ck, cv = _in_proj(x, mod[l], b_off, lw, tabs, tm)
        oa = _dense_attention(_attn_a_kernel, aq, ak, av, A_Q, tq, "attn_a")
        ob = _na_attention(bq, bk, bv, na_bias[l])
        oc = _dense_attention(_attn_c_kernel, cq, ck, cv, C_VW, tq, "attn_c")
        x1, h2 = _out_proj(oa, ob, oc, x, mod[l], b_off, lw, tm)
        h2 = h2.reshape(T, D_MODEL)
        n_chunk = PEER_CHUNKS
        chunk = T // n_chunk
        tiles = chunk // tp
        sel = _peer_select(h2, lw, tp, 0, chunk)
        eidx, w = [], []
        for c in range(n_chunk):
            e_c, g_c = sel[0].T, sel[1].T
            if c + 1 < n_chunk:
                w_c, *sel = _peer_u_select(e_c, h2, g_c, lw["utab"], tabs, lw, tp, c * tiles)
            else:
                w_c = _peer_u(e_c, h2, g_c, lw["utab"], tabs, tp, c * tiles)
            eidx.append(e_c)
            w.append(w_c)
        eidx, w = jnp.concatenate(eidx, axis=0), jnp.concatenate(w, axis=0)
        x = _peer_v(eidx, w, x1.reshape(T, D_MODEL), mod[l], (b_off, B), final_g[None, :],
                    lw["vtab"], tp, l == depth - 1).reshape(B, S, D_MODEL)
    return x


def kernel(x_prompt, x_sample, c_prompt, c_sample, ada_w, ada_b, norm1_g, norm2_g, w_in, a_q_norm, a_k_norm, b_rpb, c_q_norm, c_wqb, c_kv_norm, c_wkvb, w_out, peer_wq, peer_subkeys, peer_u, peer_v, final_g):
    depth = ada_w.shape[0]
    c_all = jnp.concatenate([c_prompt, c_sample], axis=0)
    mod = _modulation(c_all, ada_w, ada_b).reshape(depth, c_all.shape[0], 6, D_MODEL)
    layers = [_layer_weights(l, norm1_g, norm2_g, w_in, a_q_norm, a_k_norm, c_q_norm, c_wqb, c_kv_norm,
                             c_wkvb, w_out, peer_wq, peer_subkeys, peer_u, peer_v) for l in range(depth)]
    outs = []
    b_off = 0
    for x in (x_prompt, x_sample):
        rows = x.shape[1] // GRID_W
        na_bias = [_na_bias(b_rpb[l], rows) for l in range(depth)]
        outs.append(_trunk(x, mod, b_off, c_all.shape[0], layers, na_bias, final_g))
        b_off += x.shape[0]
    return tuple(outs)
```

```python
import functools

import numpy as np
import jax
import jax.numpy as jnp
from jax import lax
from jax.experimental import pallas as pl
from jax.experimental.pallas import tpu as pltpu

D_MODEL = 1024
GRID_W = 64
HEAD_DIM = 64
EPS = 1e-6
ROPE_THETA = 10000.0
A_HEADS = 6
A_KV_HEADS = 2
B_HEADS = 5
NA_ROWS = 8
NA_COLS = 16
C_HEADS = 5
C_NOPE = 64
C_ROPE = 32
C_V = 64
C_Q_RANK = 256
C_KV_RANK = 128
PEER_HEADS = 8
N_KEYS = 128
PEER_TOPK = 16
PEER_DKEY = 128
N_SEL = PEER_HEADS * PEER_TOPK

A_Q = A_HEADS * HEAD_DIM
A_KV = A_KV_HEADS * HEAD_DIM
B_W = B_HEADS * HEAD_DIM
C_QW = C_HEADS * 128
C_VW = C_HEADS * C_V

OFF_AQ, OFF_AK, OFF_AV = 0, 384, 512
OFF_BQ, OFF_BK, OFF_BV = 640, 1024, 1408
OFF_CQA, OFF_CKV, OFF_KPE = 1792, 2048, 2176
IN_W = 2304

NA_WIN_ROWS = NA_ROWS + 1
NA_KEYS = NA_WIN_ROWS * GRID_W
NEG_BIG = -1e30

V7X_VMEM_BYTES = 64 * 1024 * 1024
VMEM_LIMIT = 56 * 1024 * 1024

BF16 = jnp.bfloat16
F32 = jnp.float32


def _dot(a, b):
    return jnp.dot(a, b, preferred_element_type=F32)


def _dot_nt(a, b):
    return lax.dot_general(a, b, (((1,), (1,)), ((), ())), preferred_element_type=F32)


def _dot_split(a, m):
    hi = a.astype(BF16)
    lo = (a - hi.astype(F32)).astype(BF16)
    return _dot(hi, m) + _dot(lo, m)


def _rms(x):
    return x * lax.rsqrt(jnp.mean(x * x, axis=-1, keepdims=True) + EPS)


def _rope(x, cos, sin_signed, half):
    n = x.shape[1]
    fwd = pltpu.roll(x, n - half, 1)
    bwd = pltpu.roll(x, half, 1)
    lane = lax.broadcasted_iota(jnp.int32, x.shape, 1)
    partner = jnp.where((lane % (2 * half)) < half, fwd, bwd)
    return x * cos + partner * sin_signed


def _params(sem):
    return pltpu.CompilerParams(dimension_semantics=sem, vmem_limit_bytes=VMEM_LIMIT)


def _mod_kernel(c_ref, w_ref, b_ref, o_ref):
    c = c_ref[...]
    sc = (c * jax.nn.sigmoid(c)).astype(BF16)
    o_ref[0] = _dot(sc, w_ref[0].astype(BF16)) + b_ref[0]


def _modulation(c_all, ada_w, ada_b):
    depth = ada_w.shape[0]
    nb = c_all.shape[0]
    tn = 1536
    return pl.pallas_call(
        _mod_kernel,
        grid=(depth, 6 * D_MODEL // tn),
        in_specs=[
            pl.BlockSpec((nb, D_MODEL), lambda l, j: (0, 0)),
            pl.BlockSpec((1, D_MODEL, tn), lambda l, j: (l, 0, j)),
            pl.BlockSpec((1, 1, tn), lambda l, j: (l, 0, j)),
        ],
        out_specs=pl.BlockSpec((1, nb, tn), lambda l, j: (l, 0, j)),
        out_shape=jax.ShapeDtypeStruct((depth, nb, 6 * D_MODEL), F32),
        compiler_params=_params(("arbitrary", "arbitrary")),
        name="modulation",
    )(c_all, ada_w, ada_b.reshape(depth, 1, 6 * D_MODEL))


def _in_proj_kernel(x_ref, mod_ref, g1_ref, w1_ref, aqn_ref, akn_ref, cqn_ref, ckvn_ref,
                    wqb_ref, wkn_ref, wkv_ref, epl_ref, m64_ref,
                    cosa_ref, sina_ref, cosc_ref, sinc_ref, cosk_ref, sink_ref,
                    aq_ref, ak_ref, av_ref, bq_ref, bk_ref, bv_ref, cq_ref, ck_ref, cv_ref):
    x = x_ref[0]
    shift, scale = mod_ref[0:1, :], mod_ref[1:2, :]
    h = _rms(x) * g1_ref[...]
    h = h * (1.0 + scale) + shift
    z = _dot(h.astype(BF16), w1_ref[...])

    m64 = m64_ref[...]
    zq = z[:, OFF_AQ:OFF_AQ + A_Q]
    msq = _dot_split(zq * zq, m64) * (1.0 / HEAD_DIM)
    qn = zq * lax.rsqrt(msq + EPS) * aqn_ref[...]
    qn = _rope(qn, cosa_ref[...], sina_ref[...], HEAD_DIM // 4)
    aq_ref[0] = (qn * (HEAD_DIM ** -0.5)).astype(BF16)

    zk = z[:, OFF_AK:OFF_AK + A_KV]
    msk = _dot_split(zk * zk, m64[:A_KV, :A_KV]) * (1.0 / HEAD_DIM)
    kn = zk * lax.rsqrt(msk + EPS) * akn_ref[...]
    kn = _rope(kn, cosa_ref[:, :A_KV], sina_ref[:, :A_KV], HEAD_DIM // 4)
    ak_ref[0] = kn.astype(BF16)
    av_ref[0] = z[:, OFF_AV:OFF_AV + A_KV].astype(BF16)

    bq_ref[0] = (z[:, OFF_BQ:OFF_BQ + B_W] * (HEAD_DIM ** -0.5)).astype(BF16)
    bk_ref[0] = z[:, OFF_BK:OFF_BK + B_W].astype(BF16)
    bv_ref[0] = z[:, OFF_BV:OFF_BV + B_W].astype(BF16)

    cqa = _rms(z[:, OFF_CQA:OFF_CQA + C_Q_RANK]) * cqn_ref[...]
    cq = _dot(cqa.astype(BF16), wqb_ref[...])
    cq = _rope(cq, cosc_ref[...], sinc_ref[...], C_ROPE // 4)
    cq_ref[0] = (cq * ((C_NOPE + C_ROPE) ** -0.5)).astype(BF16)

    ckv = (_rms(z[:, OFF_CKV:OFF_CKV + C_KV_RANK]) * ckvn_ref[...]).astype(BF16)
    kpe = _rope(z[:, OFF_KPE:OFF_KPE + 128], cosk_ref[...], sink_ref[...], C_ROPE // 4)
    ck = _dot(ckv, wkn_ref[...]) + _dot(kpe.astype(BF16), epl_ref[...])
    ck_ref[0] = ck.astype(BF16)
    cv_ref[0] = _dot(ckv, wkv_ref[...]).astype(BF16)


def _in_proj(x, mod_l, b_off, lw, tabs, tm):
    B, S, _ = x.shape
    nt = S // tm
    tok = lambda w: pl.BlockSpec((1, tm, w), lambda b, i: (b, i, 0))
    full = lambda a: pl.BlockSpec(a.shape, lambda b, i: (0,) * a.ndim)
    tab = lambda w: pl.BlockSpec((tm, w), lambda b, i: (i, 0))
    widths = (A_Q, A_KV, A_KV, B_W, B_W, B_W, C_QW, C_QW, C_VW)
    consts = (lw["g1"], lw["w1"], lw["aqn"], lw["akn"], lw["cqn"], lw["ckvn"],
              lw["wqb"], lw["wkn"], lw["wkv"], tabs["epl"], tabs["m64"])
    return pl.pallas_call(
        _in_proj_kernel,
        grid=(B, nt),
        in_specs=[tok(D_MODEL),
                  pl.BlockSpec((None, 6, D_MODEL), lambda b, i: (b + b_off, 0, 0))]
                 + [full(a) for a in consts]
                 + [tab(A_Q), tab(A_Q), tab(C_QW), tab(C_QW), tab(128), tab(128)],
        out_specs=[tok(w) for w in widths],
        out_shape=[jax.ShapeDtypeStruct((B, S, w), BF16) for w in widths],
        compiler_params=_params(("parallel", "parallel")),
        name="in_proj",
    )(x, mod_l, *consts, tabs["cosa"], tabs["sina"], tabs["cosc"], tabs["sinc"],
      tabs["cosk"], tabs["sink"])


def _softmax_pv(s, v):
    m = jnp.max(s, axis=-1, keepdims=True)
    p = jnp.exp(s - m)
    l = jnp.sum(p, axis=-1, keepdims=True)
    return _dot(p.astype(BF16), v) / l


def _attn_a_kernel(q_ref, k_ref, v_ref, o_ref):
    group = A_HEADS // A_KV_HEADS
    outs = []
    for g in range(A_KV_HEADS):
        k = k_ref[0, :, g * HEAD_DIM:(g + 1) * HEAD_DIM]
        v = v_ref[0, :, g * HEAD_DIM:(g + 1) * HEAD_DIM]
        for j in range(group):
            h = g * group + j
            q = q_ref[0, :, h * HEAD_DIM:(h + 1) * HEAD_DIM]
            outs.append(_softmax_pv(_dot_nt(q, k), v))
    o_ref[0] = jnp.concatenate(outs, axis=-1).astype(BF16)


def _attn_c_kernel(q_ref, k_ref, v_ref, o_ref):
    outs = []
    for h in range(C_HEADS):
        q = q_ref[0, :, h * 128:(h + 1) * 128]
        k = k_ref[0, :, h * 128:(h + 1) * 128]
        v = v_ref[0, :, h * C_V:(h + 1) * C_V]
        outs.append(_softmax_pv(_dot_nt(q, k), v))
    o_ref[0] = jnp.concatenate(outs, axis=-1).astype(BF16)


def _dense_attention(kernel, q, k, v, out_w, tq, name):
    B, S, _ = q.shape
    return pl.pallas_call(
        kernel,
        grid=(B, S // tq),
        in_specs=[pl.BlockSpec((1, tq, q.shape[2]), lambda b, i: (b, i, 0)),
                  pl.BlockSpec((1, S, k.shape[2]), lambda b, i: (b, 0, 0)),
                  pl.BlockSpec((1, S, v.shape[2]), lambda b, i: (b, 0, 0))],
        out_specs=pl.BlockSpec((1, tq, out_w), lambda b, i: (b, i, 0)),
        out_shape=jax.ShapeDtypeStruct((B, S, out_w), BF16),
        compiler_params=_params(("parallel", "arbitrary")),
        name=name,
    )(q, k, v)


def _na_case(i, nblk):
    return jnp.where(i == 0, 0, jnp.where(i == 1, 1, jnp.where(i == nblk - 2, 3,
                     jnp.where(i == nblk - 1, 4, 2))))


def _attn_b_kernel(q_ref, k_ref, v_ref, bias_ref, o_ref, *, rows):
    i = pl.program_id(1)
    wstart = jnp.clip(2 * i - NA_ROWS // 2, 0, rows - NA_WIN_ROWS)
    start = pl.multiple_of(wstart * GRID_W, GRID_W)
    kw = k_ref[0, pl.ds(start, NA_KEYS), :]
    vw = v_ref[0, pl.ds(start, NA_KEYS), :]
    sls = [slice(h * HEAD_DIM, (h + 1) * HEAD_DIM) for h in range(B_HEADS)]
    s = [_dot_nt(q_ref[0, :, sl], kw[:, sl]) + bias_ref[h] for h, sl in enumerate(sls)]
    m = [jnp.max(x, axis=-1, keepdims=True) for x in s]
    p = [jnp.exp(x - mx) for x, mx in zip(s, m)]
    l = [jnp.sum(x, axis=-1, keepdims=True) for x in p]
    o = [_dot(x.astype(BF16), vw[:, sl]) for x, sl in zip(p, sls)]
    o_ref[0] = jnp.concatenate([x / lx for x, lx in zip(o, l)], axis=-1).astype(BF16)


def _na_attention(q, k, v, bias):
    B, S, _ = q.shape
    rows = S // GRID_W
    nblk = rows // 2
    tq = 2 * GRID_W
    return pl.pallas_call(
        functools.partial(_attn_b_kernel, rows=rows),
        grid=(B, nblk),
        in_specs=[pl.BlockSpec((1, tq, B_W), lambda b, i: (b, i, 0)),
                  pl.BlockSpec((1, S, B_W), lambda b, i: (b, 0, 0)),
                  pl.BlockSpec((1, S, B_W), lambda b, i: (b, 0, 0)),
                  pl.BlockSpec((None, B_HEADS, tq, NA_KEYS),
                               lambda b, i: (_na_case(i, nblk), 0, 0, 0))],
        out_specs=pl.BlockSpec((1, tq, B_W), lambda b, i: (b, i, 0)),
        out_shape=jax.ShapeDtypeStruct((B, S, B_W), BF16),
        compiler_params=_params(("parallel", "arbitrary")),
        name="attn_b",
    )(q, k, v, bias)


def _na_bias(rpb, rows):
    assert rows >= NA_ROWS + 4 and rows % 2 == 0
    nblk = rows // 2
    wr = min(NA_ROWS, rows)
    qc = np.arange(GRID_W)
    cs = np.clip(qc - NA_COLS // 2, 0, GRID_W - NA_COLS)
    in_c = (qc[None, :] >= cs[:, None]) & (qc[None, :] < cs[:, None] + NA_COLS)
    cidx = qc[None, :] - qc[:, None] + NA_COLS - 1
    toeplitz = (cidx[None] == np.arange(2 * NA_COLS - 1)[:, None, None]) & in_c[None]
    ridx = np.zeros((5, 2, NA_WIN_ROWS), np.int64)
    in_r = np.zeros((5, 2, NA_WIN_ROWS), bool)
    for case, blk in enumerate((0, 1, 2, nblk - 2, nblk - 1)):
        wstart = int(np.clip(2 * blk - NA_ROWS // 2, 0, rows - NA_WIN_ROWS))
        for j in range(2):
            qr = 2 * blk + j
            rs = int(np.clip(qr - wr // 2, 0, rows - wr))
            kr = wstart + np.arange(NA_WIN_ROWS)
            in_r[case, j] = (kr >= rs) & (kr < rs + wr)
            ridx[case, j] = np.clip(kr - qr + NA_ROWS - 1, 0, 2 * NA_ROWS - 2)
    picked = rpb[:, ridx, :]
    bias = jnp.einsum("hcjkd,dxy->chjxky", picked, jnp.asarray(toeplitz, F32),
                      precision=lax.Precision.HIGHEST)
    mask = in_r[:, None, :, None, :, None] & in_c[None, None, None, :, None, :]
    bias = jnp.where(mask, bias, NEG_BIG)
    return bias.reshape(5, rpb.shape[0], 2 * GRID_W, NA_KEYS)


def _out_proj_kernel(oa_ref, ob_ref, oc_ref, x_ref, mod_ref, g2_ref, w_ref, x1_ref, h2_ref):
    y = (_dot(oa_ref[0], w_ref[0:A_Q, :])
         + _dot(ob_ref[0], w_ref[A_Q:A_Q + B_W, :])
         + _dot(oc_ref[0], w_ref[A_Q + B_W:, :]))
    x1 = x_ref[0] + mod_ref[2:3, :] * y
    x1_ref[0] = x1
    h2 = _rms(x1) * g2_ref[...]
    h2_ref[0] = (h2 * (1.0 + mod_ref[4:5, :]) + mod_ref[3:4, :]).astype(BF16)


def _out_proj(oa, ob, oc, x, mod_l, b_off, lw, tm):
    B, S, _ = x.shape
    tok = lambda w: pl.BlockSpec((1, tm, w), lambda b, i: (b, i, 0))
    return pl.pallas_call(
        _out_proj_kernel,
        grid=(B, S // tm),
        in_specs=[tok(A_Q), tok(B_W), tok(C_VW), tok(D_MODEL),
                  pl.BlockSpec((None, 6, D_MODEL), lambda b, i: (b + b_off, 0, 0)),
                  pl.BlockSpec((1, D_MODEL), lambda b, i: (0, 0)),
                  pl.BlockSpec((D_MODEL, D_MODEL), lambda b, i: (0, 0))],
        out_specs=[tok(D_MODEL), tok(D_MODEL)],
        out_shape=[jax.ShapeDtypeStruct((B, S, D_MODEL), F32),
                   jax.ShapeDtypeStruct((B, S, D_MODEL), BF16)],
        compiler_params=_params(("parallel", "parallel")),
        name="out_proj",
    )(oa, ob, oc, x, mod_l, lw["g2"], lw["wout"])


def _extract_max(s, order, n):
    m = jnp.max(s, axis=0, keepdims=True)
    am = jnp.min(jnp.where(s == m, order, n), axis=0, keepdims=True)
    return m, am, jnp.where(order == am, -jnp.inf, s)


def _drain(steps):
    for _ in steps:
        pass


def _alternate(*streams):
    streams = list(streams)
    while streams:
        for entry in list(streams):
            g, per_turn = entry
            for _ in range(per_turn):
                if next(g, StopIteration) is StopIteration:
                    streams.remove(entry)
                    break


def _select_head_fn(qt_scr, sk_ref, eidx_ref, gate_ref, tm):
    lanes = 128
    iota_k = lax.broadcasted_iota(jnp.int32, (N_KEYS, lanes), 0)
    half = PEER_TOPK // 2
    dk = PEER_DKEY // 2
    iota8 = lax.broadcasted_iota(jnp.int32, (half, lanes), 0)
    flat = jnp.concatenate([iota8 * PEER_TOPK + b for b in range(half)]
                           + [iota8 + half, (iota8 + half) * PEER_TOPK], axis=0)
    n_flat = PEER_TOPK * PEER_TOPK

    def scores(hp, lo):
        q = qt_scr[pl.ds(pl.multiple_of(hp * dk, dk), dk), lo:lo + lanes]
        return _dot(sk_ref[hp], q)

    def head(h):
        for lo in range(0, tm, lanes):
            yield from head_pass(h, lo)

    def head_pass(h, lo):
        s = [scores(2 * h, lo), scores(2 * h + 1, lo)]
        vals, idxs = ([], []), ([], [])
        yield
        for _ in range(PEER_TOPK):
            for p in range(2):
                m, am, s[p] = _extract_max(s[p], iota_k, N_KEYS)
                vals[p].append(m)
                idxs[p].append(am)
            yield
        v0, v1 = (jnp.concatenate(v, axis=0) for v in vals)
        i0, i1 = (jnp.concatenate(i, axis=0) for i in idxs)
        cand, cidx = [], []
        for b in range(half):
            c = v0[:half] + v1[b:b + 1]
            a_max = PEER_TOPK // (b + 1) - 1
            cand.append(c if a_max >= half - 1 else jnp.where(iota8 <= a_max, c, -jnp.inf))
            cidx.append(i0[:half] * N_KEYS + i1[b:b + 1])
        cand += [v0[0:1] + v1[half:], v0[half:] + v1[0:1]]
        cidx += [i0[0:1] * N_KEYS + i1[half:], i0[half:] * N_KEYS + i1[0:1]]
        cand = jnp.concatenate(cand, axis=0)
        cidx = jnp.concatenate(cidx, axis=0)
        yield
        tv, te = [], []
        for _ in range(PEER_TOPK):
            m, am, cand = _extract_max(cand, flat, n_flat)
            te.append(jnp.sum(jnp.where(flat == am, cidx, 0), axis=0, keepdims=True))
            tv.append(m)
            yield
        tv = jnp.concatenate(tv, axis=0)
        ex = jnp.exp(tv - tv[0:1])
        row = pl.multiple_of(h * PEER_TOPK, PEER_TOPK)
        gate_ref[pl.ds(row, PEER_TOPK), lo:lo + lanes] = ex / jnp.sum(ex, axis=0, keepdims=True)
        eidx_ref[pl.ds(row, PEER_TOPK), lo:lo + lanes] = jnp.concatenate(te, axis=0) * 4

    return head


def _peer_select_kernel(h2_ref, wqt_ref, sk_ref, eidx_ref, gate_ref, qt_scr):
    tm = h2_ref.shape[0]
    qt_scr[...] = _dot_nt(wqt_ref[...], h2_ref[...]).astype(BF16)
    head = _select_head_fn(qt_scr, sk_ref, eidx_ref, gate_ref, tm)

    def head_body(h, carry):
        _drain(head(h))
        return carry

    lax.fori_loop(0, PEER_HEADS, head_body, 0)


def _select_specs(tm, tile_off):
    in_specs = [pl.BlockSpec((tm, D_MODEL), lambda i: (i + tile_off, 0)),
                pl.BlockSpec((D_MODEL, D_MODEL), lambda i: (0, 0)),
                pl.BlockSpec((2 * PEER_HEADS, N_KEYS, PEER_DKEY // 2), lambda i: (0, 0, 0))]
    out_specs = [pl.BlockSpec((N_SEL, tm), lambda i: (0, i)), pl.BlockSpec((N_SEL, tm), lambda i: (0, i))]
    out_shape = lambda n: [jax.ShapeDtypeStruct((N_SEL, n), jnp.int32), jax.ShapeDtypeStruct((N_SEL, n), F32)]
    return in_specs, out_specs, out_shape, pltpu.VMEM((D_MODEL, tm), BF16)


def _peer_select(h2, lw, tm, tile_off, n_tok):
    in_specs, out_specs, out_shape, qt = _select_specs(tm, tile_off)
    return pl.pallas_call(
        _peer_select_kernel,
        grid=(n_tok // tm,),
        in_specs=in_specs,
        out_specs=out_specs,
        out_shape=out_shape(n_tok),
        scratch_shapes=[qt],
        compiler_params=_params(("parallel",)),
        name="peer_select",
    )(h2, lw["wqt"], lw["sk"])


ROW_WORDS = D_MODEL // 2 // 128
GATHER_STRIDE = N_SEL + 1
TOK_GROUP = 16
GROUPS_PER_TRIP = 4
PEER_CHUNKS = 8


def _gather_rows(idx_ref, t, tab_ref, tile_ref):
    for k in range(N_SEL):
        e4 = pl.multiple_of(idx_ref[t, k], ROW_WORDS)
        tile_ref[pl.ds(k, ROW_WORDS, stride=GATHER_STRIDE), :] = tab_ref[pl.ds(e4, ROW_WORDS), :]


def _tile_chunk(tile_ref, j):
    return pltpu.bitcast(tile_ref[pl.ds(j * GATHER_STRIDE, N_SEL), :], BF16)


def _u_group_fn(idx_ref, h2_ref, tab_ref, tiles, comb_scr):
    row_iota = lax.broadcasted_iota(jnp.int32, (TOK_GROUP, D_MODEL), 0)
    lane = lax.broadcasted_iota(jnp.int32, (TOK_GROUP, 2 * N_SEL), 1)

    def group(base):
        xs = h2_ref[pl.ds(base, TOK_GROUP), :].astype(F32)
        acc = None
        for i in range(TOK_GROUP):
            tile = tiles[i % 2]
            _gather_rows(idx_ref, base + i, tab_ref, tile)
            w = jnp.concatenate([_tile_chunk(tile, j) for j in range(ROW_WORDS)], axis=-1)
            xm = jnp.where(row_iota == i, xs, 0.0)
            lhs = jnp.concatenate([xm[:, :D_MODEL // 2], xm[:, D_MODEL // 2:]], axis=0)
            d = _dot_nt(lhs.astype(BF16), w)
            acc = d if acc is None else acc + d
            yield
        comb_scr[pl.ds(base, TOK_GROUP), :] = jnp.where(lane % 2 == 0, acc[:TOK_GROUP], acc[TOK_GROUP:])

    return group


def _u_finish(comb_scr, gate_ref, ssum_ref, pa_ref, pb_ref, w_ref):
    a = _dot_split(comb_scr[...], ssum_ref[...])
    act = 0.5 * a * (1.0 + lax.erf(a * (2.0 ** -0.5)))
    w = (gate_ref[...] * act).astype(BF16)
    w_ref[:, :2 * N_SEL] = _dot(w, pa_ref[...]).astype(BF16)
    w_ref[:, 2 * N_SEL:] = _dot(w, pb_ref[...]).astype(BF16)


def _peer_u_kernel(idx_ref, h2_ref, gate_ref, tab_ref, ssum_ref, pa_ref, pb_ref, w_ref,
                   tile0_scr, tile1_scr, comb_scr):
    group = _u_group_fn(idx_ref, h2_ref, tab_ref, (tile0_scr, tile1_scr), comb_scr)

    def body(g, carry):
        for j in range(GROUPS_PER_TRIP):
            _drain(group(pl.multiple_of((g * GROUPS_PER_TRIP + j) * TOK_GROUP, TOK_GROUP)))
        return carry

    lax.fori_loop(0, h2_ref.shape[0] // (GROUPS_PER_TRIP * TOK_GROUP), body, 0)
    _u_finish(comb_scr, gate_ref, ssum_ref, pa_ref, pb_ref, w_ref)


def _peer_u_select_kernel(idx_ref, h2_ref, gate_ref, tab_ref, ssum_ref, pa_ref, pb_ref,
                          h2n_ref, wqt_ref, sk_ref, w_ref, eidx_ref, gaten_ref,
                          tile0_scr, tile1_scr, comb_scr, qt_scr):
    tm = h2_ref.shape[0]
    per_head = tm // PEER_HEADS
    qt_scr[...] = _dot_nt(wqt_ref[...], h2n_ref[...]).astype(BF16)
    group = _u_group_fn(idx_ref, h2_ref, tab_ref, (tile0_scr, tile1_scr), comb_scr)
    head = _select_head_fn(qt_scr, sk_ref, eidx_ref, gaten_ref, tm)

    def gather_steps(h):
        for g in range(per_head // TOK_GROUP):
            yield from group(pl.multiple_of(h * per_head + g * TOK_GROUP, TOK_GROUP))

    def body(h, carry):
        _alternate((gather_steps(h), 1), (head(h), tm // 128))
        return carry

    lax.fori_loop(0, PEER_HEADS, body, 0)
    _u_finish(comb_scr, gate_ref, ssum_ref, pa_ref, pb_ref, w_ref)


def _peer_v_kernel(idx_ref, w_ref, x1_ref, mod_ref, fg_ref, tab_ref, o_ref, tile0_scr, tile1_scr, *, final):
    tm = w_ref.shape[0]
    row_iota = lax.broadcasted_iota(jnp.int32, (TOK_GROUP, 4 * N_SEL), 0)
    g2 = mod_ref[5:6, :]
    tiles = (tile0_scr, tile1_scr)

    def group(base):
        ws = w_ref[pl.ds(base, TOK_GROUP), :].astype(F32)
        accs = None
        for i in range(TOK_GROUP):
            tile = tiles[i % 2]
            _gather_rows(idx_ref, base + i, tab_ref, tile)
            wm = jnp.where(row_iota == i, ws, 0.0)
            lhs = jnp.concatenate([wm[:, :2 * N_SEL], wm[:, 2 * N_SEL:]], axis=0).astype(BF16)
            ds = [_dot(lhs, _tile_chunk(tile, j)) for j in range(ROW_WORDS)]
            accs = ds if accs is None else [a + d for a, d in zip(accs, ds)]
        y = jnp.concatenate([a[:TOK_GROUP] for a in accs] + [a[TOK_GROUP:] for a in accs], axis=-1)
        x2 = x1_ref[pl.ds(base, TOK_GROUP), :] + g2 * y
        if final:
            x2 = _rms(x2) * fg_ref[...]
        o_ref[pl.ds(base, TOK_GROUP), :] = x2

    def body(g, carry):
        for j in range(GROUPS_PER_TRIP):
            group(pl.multiple_of((g * GROUPS_PER_TRIP + j) * TOK_GROUP, TOK_GROUP))
        return carry

    lax.fori_loop(0, tm // (GROUPS_PER_TRIP * TOK_GROUP), body, 0)


def _table_spec(tab):
    return pl.BlockSpec(tab.shape, lambda i: (0, 0), pipeline_mode=pl.Buffered(1))


def _u_specs(tm, tile_off, utab):
    in_specs = [pl.BlockSpec((tm, N_SEL), lambda i: (i, 0), memory_space=pltpu.SMEM),
                pl.BlockSpec((tm, D_MODEL), lambda i: (i + tile_off, 0)),
                pl.BlockSpec((tm, N_SEL), lambda i: (i, 0)),
                _table_spec(utab),
                pl.BlockSpec((2 * N_SEL, N_SEL), lambda i: (0, 0)),
                pl.BlockSpec((N_SEL, 2 * N_SEL), lambda i: (0, 0)),
                pl.BlockSpec((N_SEL, 2 * N_SEL), lambda i: (0, 0))]
    scratch = [pltpu.VMEM((ROW_WORDS * GATHER_STRIDE, 128), jnp.uint32),
               pltpu.VMEM((ROW_WORDS * GATHER_STRIDE, 128), jnp.uint32),
               pltpu.VMEM((tm, 2 * N_SEL), F32)]
    return in_specs, pl.BlockSpec((tm, 4 * N_SEL), lambda i: (i, 0)), scratch


def _peer_u(eidx, h2, gate, utab, tabs, tm, tile_off):
    n_tok = eidx.shape[0]
    in_specs, out_spec, scratch = _u_specs(tm, tile_off, utab)
    return pl.pallas_call(
        _peer_u_kernel,
        grid=(n_tok // tm,),
        in_specs=in_specs,
        out_specs=out_spec,
        out_shape=jax.ShapeDtypeStruct((n_tok, 4 * N_SEL), BF16),
        scratch_shapes=scratch,
        compiler_params=_params(("arbitrary",)),
        name="peer_u",
    )(eidx, h2, gate, utab, tabs["ssum"], tabs["pa"], tabs["pb"])


def _peer_u_select(eidx, h2, gate, utab, tabs, lw, tm, tile_off):
    n_tok = eidx.shape[0]
    in_specs, out_spec, scratch = _u_specs(tm, tile_off, utab)
    sel_in, sel_out, sel_shape, qt = _select_specs(tm, tile_off + n_tok // tm)
    return pl.pallas_call(
        _peer_u_select_kernel,
        grid=(n_tok // tm,),
        in_specs=in_specs + sel_in,
        out_specs=[out_spec] + sel_out,
        out_shape=[jax.ShapeDtypeStruct((n_tok, 4 * N_SEL), BF16)] + sel_shape(n_tok),
        scratch_shapes=scratch + [qt],
        compiler_params=_params(("arbitrary",)),
        name="peer_u_select",
    )(eidx, h2, gate, utab, tabs["ssum"], tabs["pa"], tabs["pb"], h2, lw["wqt"], lw["sk"])


def _peer_v(eidx, w, x1, mod_l, b_idx, final_g, vtab, tm, final):
    T = w.shape[0]
    tiles_per_batch = T // tm // b_idx[1]
    return pl.pallas_call(
        functools.partial(_peer_v_kernel, final=final),
        grid=(T // tm,),
        in_specs=[pl.BlockSpec((tm, N_SEL), lambda i: (i, 0), memory_space=pltpu.SMEM),
                  pl.BlockSpec((tm, 4 * N_SEL), lambda i: (i, 0)),
                  pl.BlockSpec((tm, D_MODEL), lambda i: (i, 0)),
                  pl.BlockSpec((None, 6, D_MODEL),
                               lambda i: (i // tiles_per_batch + b_idx[0], 0, 0)),
                  pl.BlockSpec((1, D_MODEL), lambda i: (0, 0)),
                  _table_spec(vtab)],
        out_specs=pl.BlockSpec((tm, D_MODEL), lambda i: (i, 0)),
        out_shape=jax.ShapeDtypeStruct((T, D_MODEL), F32),
        scratch_shapes=[pltpu.VMEM((ROW_WORDS * GATHER_STRIDE, 128), jnp.uint32),
                        pltpu.VMEM((ROW_WORDS * GATHER_STRIDE, 128), jnp.uint32)],
        compiler_params=_params(("arbitrary",)),
        name="peer_v",
    )(eidx, w, x1, mod_l, final_g, vtab)


def _pack_table(t):
    n = t.shape[0]
    b = lax.bitcast_convert_type(t.astype(BF16), jnp.uint16).astype(jnp.uint32)
    words = b[:, :D_MODEL // 2] | (b[:, D_MODEL // 2:] << 16)
    return words.reshape(n * ROW_WORDS, 128)


def _rope_lane_tables(S, dim):
    dq = dim // 4
    freqs = ROPE_THETA ** (-jnp.arange(dq, dtype=F32) / dq)
    t = jnp.arange(S, dtype=jnp.int32)
    row = (t // GRID_W).astype(F32)
    col = (t % GRID_W).astype(F32)
    ang = jnp.stack([row[:, None] * freqs, col[:, None] * freqs], axis=1)
    cos, sin = jnp.cos(ang), jnp.sin(ang)
    d = np.arange(dim)
    blk, f = d // (2 * dq), d % dq
    sign = np.where((d % (2 * dq)) < dq, -1.0, 1.0).astype(np.float32)
    return cos[:, blk, f], sin[:, blk, f] * sign


def _const_tables(S):
    cos_a, sin_a = _rope_lane_tables(S, HEAD_DIM)
    cos_c, sin_c = _rope_lane_tables(S, C_ROPE)
    one = jnp.ones((S, 128), F32)
    zero = jnp.zeros((S, 128), F32)
    cosk = one.at[:, :C_ROPE].set(cos_c)
    sink = zero.at[:, :C_ROPE].set(sin_c)
    cosh = one.at[:, C_NOPE:C_NOPE + C_ROPE].set(cos_c)
    sinh = zero.at[:, C_NOPE:C_NOPE + C_ROPE].set(sin_c)
    head_of = np.arange(A_Q) // HEAD_DIM
    epl = np.zeros((128, C_QW), np.float32)
    for h in range(C_HEADS):
        epl[np.arange(C_ROPE), h * 128 + C_NOPE + np.arange(C_ROPE)] = 1.0
    k = np.arange(N_SEL)
    ssum = np.zeros((2 * N_SEL, N_SEL), np.float32)
    ssum[2 * k, k] = 1.0
    ssum[2 * k + 1, k] = 1.0
    pa = np.zeros((N_SEL, 2 * N_SEL), np.float32)
    pb = np.zeros((N_SEL, 2 * N_SEL), np.float32)
    pa[k, 2 * k] = 1.0
    pb[k, 2 * k + 1] = 1.0
    return {
        "cosa": jnp.tile(cos_a, (1, A_HEADS)), "sina": jnp.tile(sin_a, (1, A_HEADS)),
        "cosc": jnp.tile(cosh, (1, C_HEADS)), "sinc": jnp.tile(sinh, (1, C_HEADS)),
        "cosk": cosk, "sink": sink,
        "m64": jnp.asarray(head_of[:, None] == head_of[None, :], BF16),
        "epl": jnp.asarray(epl, BF16), "ssum": jnp.asarray(ssum, BF16),
        "pa": jnp.asarray(pa, BF16), "pb": jnp.asarray(pb, BF16),
    }


def _layer_weights(l, norm1_g, norm2_g, w_in, a_q_norm, a_k_norm, c_q_norm, c_wqb, c_kv_norm, c_wkvb,
                   w_out, peer_wq, peer_subkeys, peer_u, peer_v):
    src = np.cumsum((0, A_Q, A_KV, A_KV, B_W, B_W, B_W, C_Q_RANK, C_KV_RANK))
    dst = (OFF_AQ, OFF_AK, OFF_AV, OFF_BQ, OFF_BK, OFF_BV, OFF_CQA, OFF_CKV, OFF_KPE)
    wid = (A_Q, A_KV, A_KV, B_W, B_W, B_W, C_Q_RANK, C_KV_RANK, C_ROPE)
    w1 = jnp.zeros((D_MODEL, IN_W), F32)
    for s, d, w in zip(src, dst, wid):
        w1 = w1.at[:, d:d + w].set(w_in[l][:, s:s + w])
    wqb = jnp.zeros((C_Q_RANK, C_HEADS, 128), F32).at[:, :, :C_NOPE + C_ROPE].set(
        c_wqb[l].reshape(C_Q_RANK, C_HEADS, C_NOPE + C_ROPE)).reshape(C_Q_RANK, C_QW)
    wkvb = c_wkvb[l].reshape(C_KV_RANK, C_HEADS, C_NOPE + C_V)
    wkn = jnp.zeros((C_KV_RANK, C_HEADS, 128), F32).at[:, :, :C_NOPE].set(
        wkvb[:, :, :C_NOPE]).reshape(C_KV_RANK, C_QW)
    wkv = wkvb[:, :, C_NOPE:].reshape(C_KV_RANK, C_VW)
    return {
        "g1": norm1_g[l][None, :], "g2": norm2_g[l][None, :],
        "w1": w1.astype(BF16),
        "aqn": jnp.tile(a_q_norm[l], A_HEADS)[None, :], "akn": jnp.tile(a_k_norm[l], A_KV_HEADS)[None, :],
        "cqn": c_q_norm[l][None, :], "ckvn": c_kv_norm[l][None, :],
        "wqb": wqb.astype(BF16), "wkn": wkn.astype(BF16), "wkv": wkv.astype(BF16),
        "wout": w_out[l].astype(BF16),
        "wqt": peer_wq[l].T.astype(BF16),
        "sk": peer_subkeys[l].reshape(2 * PEER_HEADS, N_KEYS, PEER_DKEY // 2).astype(BF16),
        "utab": _pack_table(peer_u[l]), "vtab": _pack_table(peer_v[l]),
    }


def _tile(n, pref):
    t = pref
    while n % t:
        t //= 2
    return t


def _trunk(x, mod, b_off, n_batch_total, layers, na_bias, final_g):
    B, S, _ = x.shape
    T = B * S
    tabs = _const_tables(S)
    tm = _tile(S, 256)
    tq = _tile(S, 256)
    tp = _tile(S, 256)
    depth = len(layers)
    for l, lw in enumerate(layers):
        aq, ak, av, bq, bk, bv, cq, ck, cv = _in_proj(x, mod[l], b_off, lw, tabs, tm)
        oa = _dense_attention(_attn_a_kernel, aq, ak, av, A_Q, tq, "attn_a")
        ob = _na_attention(bq, bk, bv, na_bias[l])
        oc = _dense_attention(_attn_c_kernel, cq, ck, cv, C_VW, tq, "attn_c")
        x1, h2 = _out_proj(oa, ob, oc, x, mod[l], b_off, lw, tm)
        h2 = h2.reshape(T, D_MODEL)
        n_chunk = PEER_CHUNKS
        chunk = T // n_chunk
        tiles = chunk // tp
        sel = _peer_select(h2, lw, tp, 0, chunk)
        eidx, w = [], []
        for c in range(n_chunk):
            e_c, g_c = sel[0].T, sel[1].T
            if c + 1 < n_chunk:
                w_c, *sel = _peer_u_select(e_c, h2, g_c, lw["utab"], tabs, lw, tp, c * tiles)
            else:
                w_c = _peer_u(e_c, h2, g_c, lw["utab"], tabs, tp, c * tiles)
            eidx.append(e_c)
            w.append(w_c)
        eidx, w = jnp.concatenate(eidx, axis=0), jnp.concatenate(w, axis=0)
        x = _peer_v(eidx, w, x1.reshape(T, D_MODEL), mod[l], (b_off, B), final_g[None, :],
                    lw["vtab"], tp, l == depth - 1).reshape(B, S, D_MODEL)
    return x


def kernel(x_prompt, x_sample, c_prompt, c_sample, ada_w, ada_b, norm1_g, norm2_g, w_in, a_q_norm, a_k_norm, b_rpb, c_q_norm, c_wqb, c_kv_norm, c_wkvb, w_out, peer_wq, peer_subkeys, peer_u, peer_v, final_g):
    depth = ada_w.shape[0]
    c_all = jnp.concatenate([c_prompt, c_sample], axis=0)
    mod = _modulation(c_all, ada_w, ada_b).reshape(depth, c_all.shape[0], 6, D_MODEL)
    layers = [_layer_weights(l, norm1_g, norm2_g, w_in, a_q_norm, a_k_norm, c_q_norm, c_wqb, c_kv_norm,
                             c_wkvb, w_out, peer_wq, peer_subkeys, peer_u, peer_v) for l in range(depth)]
    outs = []
    b_off = 0
    for x in (x_prompt, x_sample):
        rows = x.shape[1] // GRID_W
        na_bias = [_na_bias(b_rpb[l], rows) for l in range(depth)]
        outs.append(_trunk(x, mod, b_off, c_all.shape[0], layers, na_bias, final_g))
        b_off += x.shape[0]
    return tuple(outs)
```

```python
import functools

import numpy as np
import jax
import jax.numpy as jnp
from jax import lax
from jax.experimental import pallas as pl
from jax.experimental.pallas import tpu as pltpu

D_MODEL = 1024
GRID_W = 64
HEAD_DIM = 64
EPS = 1e-6
ROPE_THETA = 10000.0
A_HEADS = 6
A_KV_HEADS = 2
B_HEADS = 5
NA_ROWS = 8
NA_COLS = 16
C_HEADS = 5
C_NOPE = 64
C_ROPE = 32
C_V = 64
C_Q_RANK = 256
C_KV_RANK = 128
PEER_HEADS = 8
N_KEYS = 128
PEER_TOPK = 16
PEER_DKEY = 128
N_SEL = PEER_HEADS * PEER_TOPK

A_Q = A_HEADS * HEAD_DIM
A_KV = A_KV_HEADS * HEAD_DIM
B_W = B_HEADS * HEAD_DIM
C_QW = C_HEADS * 128
C_VW = C_HEADS * C_V

OFF_AQ, OFF_AK, OFF_AV = 0, 384, 512
OFF_BQ, OFF_BK, OFF_BV = 640, 1024, 1408
OFF_CQA, OFF_CKV, OFF_KPE = 1792, 2048, 2176
IN_W = 2304

NA_WIN_ROWS = NA_ROWS + 1
NA_KEYS = NA_WIN_ROWS * GRID_W
NEG_BIG = -1e30

V7X_VMEM_BYTES = 64 * 1024 * 1024
VMEM_LIMIT = V7X_VMEM_BYTES * 7 // 8

BF16 = jnp.bfloat16
F32 = jnp.float32


def _dot(a, b):
    return jnp.dot(a, b, preferred_element_type=F32)


def _dot_nt(a, b):
    return lax.dot_general(a, b, (((1,), (1,)), ((), ())), preferred_element_type=F32)


def _dot_split(a, m):
    hi = a.astype(BF16)
    lo = (a - hi.astype(F32)).astype(BF16)
    return _dot(hi, m) + _dot(lo, m)


def _rms(x):
    return x * lax.rsqrt(jnp.mean(x * x, axis=-1, keepdims=True) + EPS)


def _rope(x, cos, sin_signed, half):
    n = x.shape[1]
    fwd = pltpu.roll(x, n - half, 1)
    bwd = pltpu.roll(x, half, 1)
    lane = lax.broadcasted_iota(jnp.int32, x.shape, 1)
    partner = jnp.where((lane % (2 * half)) < half, fwd, bwd)
    return x * cos + partner * sin_signed


def _params(sem):
    return pltpu.CompilerParams(dimension_semantics=sem, vmem_limit_bytes=VMEM_LIMIT)


def _mod_kernel(c_ref, w_ref, b_ref, o_ref):
    c = c_ref[...]
    sc = (c * jax.nn.sigmoid(c)).astype(BF16)
    o_ref[0] = _dot(sc, w_ref[0].astype(BF16)) + b_ref[0]


def _modulation(c_all, ada_w, ada_b):
    depth = ada_w.shape[0]
    nb = c_all.shape[0]
    tn = 1536
    return pl.pallas_call(
        _mod_kernel,
        grid=(depth, 6 * D_MODEL // tn),
        in_specs=[
            pl.BlockSpec((nb, D_MODEL), lambda l, j: (0, 0)),
            pl.BlockSpec((1, D_MODEL, tn), lambda l, j: (l, 0, j)),
            pl.BlockSpec((1, 1, tn), lambda l, j: (l, 0, j)),
        ],
        out_specs=pl.BlockSpec((1, nb, tn), lambda l, j: (l, 0, j)),
        out_shape=jax.ShapeDtypeStruct((depth, nb, 6 * D_MODEL), F32),
        compiler_params=_params(("arbitrary", "arbitrary")),
        name="modulation",
    )(c_all, ada_w, ada_b.reshape(depth, 1, 6 * D_MODEL))


def _in_proj_kernel(x_ref, mod_ref, g1_ref, w1_ref, aqn_ref, akn_ref, cqn_ref, ckvn_ref,
                    wqb_ref, wkn_ref, wkv_ref, epl_ref, m64_ref,
                    cosa_ref, sina_ref, cosc_ref, sinc_ref, cosk_ref, sink_ref,
                    aq_ref, ak_ref, av_ref, bq_ref, bk_ref, bv_ref, cq_ref, ck_ref, cv_ref):
    x = x_ref[0]
    shift, scale = mod_ref[0:1, :], mod_ref[1:2, :]
    h = _rms(x) * g1_ref[...]
    h = h * (1.0 + scale) + shift
    z = _dot(h.astype(BF16), w1_ref[...])

    m64 = m64_ref[...]
    zq = z[:, OFF_AQ:OFF_AQ + A_Q]
    msq = _dot_split(zq * zq, m64) * (1.0 / HEAD_DIM)
    qn = zq * lax.rsqrt(msq + EPS) * aqn_ref[...]
    qn = _rope(qn, cosa_ref[...], sina_ref[...], HEAD_DIM // 4)
    aq_ref[0] = (qn * (HEAD_DIM ** -0.5)).astype(BF16)

    zk = z[:, OFF_AK:OFF_AK + A_KV]
    msk = _dot_split(zk * zk, m64[:A_KV, :A_KV]) * (1.0 / HEAD_DIM)
    kn = zk * lax.rsqrt(msk + EPS) * akn_ref[...]
    kn = _rope(kn, cosa_ref[:, :A_KV], sina_ref[:, :A_KV], HEAD_DIM // 4)
    ak_ref[0] = kn.astype(BF16)
    av_ref[0] = z[:, OFF_AV:OFF_AV + A_KV].astype(BF16)

    bq_ref[0] = (z[:, OFF_BQ:OFF_BQ + B_W] * (HEAD_DIM ** -0.5)).astype(BF16)
    bk_ref[0] = z[:, OFF_BK:OFF_BK + B_W].astype(BF16)
    bv_ref[0] = z[:, OFF_BV:OFF_BV + B_W].astype(BF16)

    cqa = _rms(z[:, OFF_CQA:OFF_CQA + C_Q_RANK]) * cqn_ref[...]
    cq = _dot(cqa.astype(BF16), wqb_ref[...])
    cq = _rope(cq, cosc_ref[...], sinc_ref[...], C_ROPE // 4)
    cq_ref[0] = (cq * ((C_NOPE + C_ROPE) ** -0.5)).astype(BF16)

    ckv = (_rms(z[:, OFF_CKV:OFF_CKV + C_KV_RANK]) * ckvn_ref[...]).astype(BF16)
    kpe = _rope(z[:, OFF_KPE:OFF_KPE + 128], cosk_ref[...], sink_ref[...], C_ROPE // 4)
    ck = _dot(ckv, wkn_ref[...]) + _dot(kpe.astype(BF16), epl_ref[...])
    ck_ref[0] = ck.astype(BF16)
    cv_ref[0] = _dot(ckv, wkv_ref[...]).astype(BF16)


def _in_proj(x, mod_l, b_off, lw, tabs, tm):
    B, S, _ = x.shape
    nt = S // tm
    tok = lambda w: pl.BlockSpec((1, tm, w), lambda b, i: (b, i, 0))
    full = lambda a: pl.BlockSpec(a.shape, lambda b, i: (0,) * a.ndim)
    tab = lambda w: pl.BlockSpec((tm, w), lambda b, i: (i, 0))
    widths = (A_Q, A_KV, A_KV, B_W, B_W, B_W, C_QW, C_QW, C_VW)
    consts = (lw["g1"], lw["w1"], lw["aqn"], lw["akn"], lw["cqn"], lw["ckvn"],
              lw["wqb"], lw["wkn"], lw["wkv"], tabs["epl"], tabs["m64"])
    return pl.pallas_call(
        _in_proj_kernel,
        grid=(B, nt),
        in_specs=[tok(D_MODEL),
                  pl.BlockSpec((None, 6, D_MODEL), lambda b, i: (b + b_off, 0, 0))]
                 + [full(a) for a in consts]
                 + [tab(A_Q), tab(A_Q), tab(C_QW), tab(C_QW), tab(128), tab(128)],
        out_specs=[tok(w) for w in widths],
        out_shape=[jax.ShapeDtypeStruct((B, S, w), BF16) for w in widths],
        compiler_params=_params(("parallel", "parallel")),
        name="in_proj",
    )(x, mod_l, *consts, tabs["cosa"], tabs["sina"], tabs["cosc"], tabs["sinc"],
      tabs["cosk"], tabs["sink"])


def _softmax_pv(s, v):
    m = jnp.max(s, axis=-1, keepdims=True)
    p = jnp.exp(s - m)
    l = jnp.sum(p, axis=-1, keepdims=True)
    return _dot(p.astype(BF16), v) / l


def _attn_a_kernel(q_ref, k_ref, v_ref, o_ref):
    group = A_HEADS // A_KV_HEADS
    outs = []
    for g in range(A_KV_HEADS):
        k = k_ref[0, :, g * HEAD_DIM:(g + 1) * HEAD_DIM]
        v = v_ref[0, :, g * HEAD_DIM:(g + 1) * HEAD_DIM]
        for j in range(group):
            h = g * group + j
            q = q_ref[0, :, h * HEAD_DIM:(h + 1) * HEAD_DIM]
            outs.append(_softmax_pv(_dot_nt(q, k), v))
    o_ref[0] = jnp.concatenate(outs, axis=-1).astype(BF16)


def _attn_c_kernel(q_ref, k_ref, v_ref, o_ref):
    outs = []
    for h in range(C_HEADS):
        q = q_ref[0, :, h * 128:(h + 1) * 128]
        k = k_ref[0, :, h * 128:(h + 1) * 128]
        v = v_ref[0, :, h * C_V:(h + 1) * C_V]
        outs.append(_softmax_pv(_dot_nt(q, k), v))
    o_ref[0] = jnp.concatenate(outs, axis=-1).astype(BF16)


def _dense_attention(kernel, q, k, v, out_w, tq, name):
    B, S, _ = q.shape
    return pl.pallas_call(
        kernel,
        grid=(B, S // tq),
        in_specs=[pl.BlockSpec((1, tq, q.shape[2]), lambda b, i: (b, i, 0)),
                  pl.BlockSpec((1, S, k.shape[2]), lambda b, i: (b, 0, 0)),
                  pl.BlockSpec((1, S, v.shape[2]), lambda b, i: (b, 0, 0))],
        out_specs=pl.BlockSpec((1, tq, out_w), lambda b, i: (b, i, 0)),
        out_shape=jax.ShapeDtypeStruct((B, S, out_w), BF16),
        compiler_params=_params(("parallel", "arbitrary")),
        name=name,
    )(q, k, v)


def _na_case(i, nblk):
    return jnp.where(i == 0, 0, jnp.where(i == 1, 1, jnp.where(i == nblk - 2, 3,
                     jnp.where(i == nblk - 1, 4, 2))))


def _attn_b_kernel(q_ref, k_ref, v_ref, bias_ref, o_ref, *, rows):
    i = pl.program_id(1)
    wstart = jnp.clip(2 * i - NA_ROWS // 2, 0, rows - NA_WIN_ROWS)
    start = pl.multiple_of(wstart * GRID_W, GRID_W)
    kw = k_ref[0, pl.ds(start, NA_KEYS), :]
    vw = v_ref[0, pl.ds(start, NA_KEYS), :]
    sls = [slice(h * HEAD_DIM, (h + 1) * HEAD_DIM) for h in range(B_HEADS)]
    s = [_dot_nt(q_ref[0, :, sl], kw[:, sl]) + bias_ref[h] for h, sl in enumerate(sls)]
    m = [jnp.max(x, axis=-1, keepdims=True) for x in s]
    p = [jnp.exp(x - mx) for x, mx in zip(s, m)]
    l = [jnp.sum(x, axis=-1, keepdims=True) for x in p]
    o = [_dot(x.astype(BF16), vw[:, sl]) for x, sl in zip(p, sls)]
    o_ref[0] = jnp.concatenate([x / lx for x, lx in zip(o, l)], axis=-1).astype(BF16)


def _na_attention(q, k, v, bias):
    B, S, _ = q.shape
    rows = S // GRID_W
    nblk = rows // 2
    tq = 2 * GRID_W
    return pl.pallas_call(
        functools.partial(_attn_b_kernel, rows=rows),
        grid=(B, nblk),
        in_specs=[pl.BlockSpec((1, tq, B_W), lambda b, i: (b, i, 0)),
                  pl.BlockSpec((1, S, B_W), lambda b, i: (b, 0, 0)),
                  pl.BlockSpec((1, S, B_W), lambda b, i: (b, 0, 0)),
                  pl.BlockSpec((None, B_HEADS, tq, NA_KEYS),
                               lambda b, i: (_na_case(i, nblk), 0, 0, 0))],
        out_specs=pl.BlockSpec((1, tq, B_W), lambda b, i: (b, i, 0)),
        out_shape=jax.ShapeDtypeStruct((B, S, B_W), BF16),
        compiler_params=_params(("parallel", "arbitrary")),
        name="attn_b",
    )(q, k, v, bias)


def _na_bias(rpb, rows):
    assert rows >= NA_ROWS + 4 and rows % 2 == 0
    nblk = rows // 2
    wr = min(NA_ROWS, rows)
    qc = np.arange(GRID_W)
    cs = np.clip(qc - NA_COLS // 2, 0, GRID_W - NA_COLS)
    in_c = (qc[None, :] >= cs[:, None]) & (qc[None, :] < cs[:, None] + NA_COLS)
    cidx = qc[None, :] - qc[:, None] + NA_COLS - 1
    toeplitz = (cidx[None] == np.arange(2 * NA_COLS - 1)[:, None, None]) & in_c[None]
    ridx = np.zeros((5, 2, NA_WIN_ROWS), np.int64)
    in_r = np.zeros((5, 2, NA_WIN_ROWS), bool)
    for case, blk in enumerate((0, 1, 2, nblk - 2, nblk - 1)):
        wstart = int(np.clip(2 * blk - NA_ROWS // 2, 0, rows - NA_WIN_ROWS))
        for j in range(2):
            qr = 2 * blk + j
            rs = int(np.clip(qr - wr // 2, 0, rows - wr))
            kr = wstart + np.arange(NA_WIN_ROWS)
            in_r[case, j] = (kr >= rs) & (kr < rs + wr)
            ridx[case, j] = np.clip(kr - qr + NA_ROWS - 1, 0, 2 * NA_ROWS - 2)
    picked = rpb[:, ridx, :]
    bias = jnp.einsum("hcjkd,dxy->chjxky", picked, jnp.asarray(toeplitz, F32),
                      precision=lax.Precision.HIGHEST)
    mask = in_r[:, None, :, None, :, None] & in_c[None, None, None, :, None, :]
    bias = jnp.where(mask, bias, NEG_BIG)
    return bias.reshape(5, rpb.shape[0], 2 * GRID_W, NA_KEYS)


def _out_proj_kernel(oa_ref, ob_ref, oc_ref, x_ref, mod_ref, g2_ref, w_ref, x1_ref, h2_ref):
    y = (_dot(oa_ref[0], w_ref[0:A_Q, :])
         + _dot(ob_ref[0], w_ref[A_Q:A_Q + B_W, :])
         + _dot(oc_ref[0], w_ref[A_Q + B_W:, :]))
    x1 = x_ref[0] + mod_ref[2:3, :] * y
    x1_ref[0] = x1
    h2 = _rms(x1) * g2_ref[...]
    h2_ref[0] = (h2 * (1.0 + mod_ref[4:5, :]) + mod_ref[3:4, :]).astype(BF16)


def _out_proj(oa, ob, oc, x, mod_l, b_off, lw, tm):
    B, S, _ = x.shape
    tok = lambda w: pl.BlockSpec((1, tm, w), lambda b, i: (b, i, 0))
    return pl.pallas_call(
        _out_proj_kernel,
        grid=(B, S // tm),
        in_specs=[tok(A_Q), tok(B_W), tok(C_VW), tok(D_MODEL),
                  pl.BlockSpec((None, 6, D_MODEL), lambda b, i: (b + b_off, 0, 0)),
                  pl.BlockSpec((1, D_MODEL), lambda b, i: (0, 0)),
                  pl.BlockSpec((D_MODEL, D_MODEL), lambda b, i: (0, 0))],
        out_specs=[tok(D_MODEL), tok(D_MODEL)],
        out_shape=[jax.ShapeDtypeStruct((B, S, D_MODEL), F32),
                   jax.ShapeDtypeStruct((B, S, D_MODEL), BF16)],
        compiler_params=_params(("parallel", "parallel")),
        name="out_proj",
    )(oa, ob, oc, x, mod_l, lw["g2"], lw["wout"])


def _extract_max(s, order, n):
    m = jnp.max(s, axis=0, keepdims=True)
    am = jnp.min(jnp.where(s == m, order, n), axis=0, keepdims=True)
    return m, am, jnp.where(order == am, -jnp.inf, s)


def _drain(steps):
    for _ in steps:
        pass


def _alternate(*streams):
    streams = list(streams)
    while streams:
        for entry in list(streams):
            g, per_turn = entry
            for _ in range(per_turn):
                if next(g, StopIteration) is StopIteration:
                    streams.remove(entry)
                    break


def _select_head_fn(qt_scr, sk_ref, eidx_ref, gate_ref, tm):
    lanes = 128
    iota_k = lax.broadcasted_iota(jnp.int32, (N_KEYS, lanes), 0)
    half = PEER_TOPK // 2
    dk = PEER_DKEY // 2
    iota8 = lax.broadcasted_iota(jnp.int32, (half, lanes), 0)
    flat = jnp.concatenate([iota8 * PEER_TOPK + b for b in range(half)]
                           + [iota8 + half, (iota8 + half) * PEER_TOPK], axis=0)
    n_flat = PEER_TOPK * PEER_TOPK

    def scores(hp, lo):
        q = qt_scr[pl.ds(pl.multiple_of(hp * dk, dk), dk), lo:lo + lanes]
        return _dot(sk_ref[hp], q)

    def head(h):
        for lo in range(0, tm, lanes):
            yield from head_pass(h, lo)

    def head_pass(h, lo):
        vals, idxs = ([], []), ([], [])
        for p in range(2):
            s = scores(2 * h + p, lo)
            yield
            for _ in range(PEER_TOPK):
                m, am, s = _extract_max(s, iota_k, N_KEYS)
                vals[p].append(m)
                idxs[p].append(am)
                yield
        v0, v1 = (jnp.concatenate(v, axis=0) for v in vals)
        i0, i1 = (jnp.concatenate(i, axis=0) for i in idxs)
        cand, cidx = [], []
        for b in range(half):
            c = v0[:half] + v1[b:b + 1]
            a_max = PEER_TOPK // (b + 1) - 1
            cand.append(c if a_max >= half - 1 else jnp.where(iota8 <= a_max, c, -jnp.inf))
            cidx.append(i0[:half] * N_KEYS + i1[b:b + 1])
        cand += [v0[0:1] + v1[half:], v0[half:] + v1[0:1]]
        cidx += [i0[0:1] * N_KEYS + i1[half:], i0[half:] * N_KEYS + i1[0:1]]
        cand = jnp.concatenate(cand, axis=0)
        cidx = jnp.concatenate(cidx, axis=0)
        yield
        tv, te = [], []
        for _ in range(PEER_TOPK):
            m, am, cand = _extract_max(cand, flat, n_flat)
            te.append(jnp.sum(jnp.where(flat == am, cidx, 0), axis=0, keepdims=True))
            tv.append(m)
            yield
        tv = jnp.concatenate(tv, axis=0)
        ex = jnp.exp(tv - tv[0:1])
        row = pl.multiple_of(h * PEER_TOPK, PEER_TOPK)
        gate_ref[pl.ds(row, PEER_TOPK), lo:lo + lanes] = ex / jnp.sum(ex, axis=0, keepdims=True)
        eidx_ref[pl.ds(row, PEER_TOPK), lo:lo + lanes] = jnp.concatenate(te, axis=0) * 4

    return head


def _peer_select_kernel(h2_ref, wqt_ref, sk_ref, eidx_ref, gate_ref, qt_scr):
    tm = h2_ref.shape[0]
    qt_scr[...] = _dot_nt(wqt_ref[...], h2_ref[...]).astype(BF16)
    head = _select_head_fn(qt_scr, sk_ref, eidx_ref, gate_ref, tm)

    def head_body(h, carry):
        _drain(head(h))
        return carry

    lax.fori_loop(0, PEER_HEADS, head_body, 0)


def _select_specs(tm, tile_off):
    in_specs = [pl.BlockSpec((tm, D_MODEL), lambda i: (i + tile_off, 0)),
                pl.BlockSpec((D_MODEL, D_MODEL), lambda i: (0, 0)),
                pl.BlockSpec((2 * PEER_HEADS, N_KEYS, PEER_DKEY // 2), lambda i: (0, 0, 0))]
    out_specs = [pl.BlockSpec((N_SEL, tm), lambda i: (0, i)), pl.BlockSpec((N_SEL, tm), lambda i: (0, i))]
    out_shape = lambda n: [jax.ShapeDtypeStruct((N_SEL, n), jnp.int32), jax.ShapeDtypeStruct((N_SEL, n), F32)]
    return in_specs, out_specs, out_shape, pltpu.VMEM((D_MODEL, tm), BF16)


def _peer_select(h2, lw, tm, tile_off, n_tok):
    in_specs, out_specs, out_shape, qt = _select_specs(tm, tile_off)
    return pl.pallas_call(
        _peer_select_kernel,
        grid=(n_tok // tm,),
        in_specs=in_specs,
        out_specs=out_specs,
        out_shape=out_shape(n_tok),
        scratch_shapes=[qt],
        compiler_params=_params(("parallel",)),
        name="peer_select",
    )(h2, lw["wqt"], lw["sk"])


ROW_WORDS = D_MODEL // 2 // 128
GATHER_STRIDE = N_SEL + 1
TOK_GROUP = 16
GROUPS_PER_TRIP = 4
PEER_CHUNKS = 8


def _gather_rows(idx_ref, t, tab_ref, tile_ref):
    for k in range(N_SEL):
        e4 = pl.multiple_of(idx_ref[t, k], ROW_WORDS)
        tile_ref[pl.ds(k, ROW_WORDS, stride=GATHER_STRIDE), :] = tab_ref[pl.ds(e4, ROW_WORDS), :]


def _tile_chunk(tile_ref, j):
    return pltpu.bitcast(tile_ref[pl.ds(j * GATHER_STRIDE, N_SEL), :], BF16)


def _u_group_fn(idx_ref, h2_ref, tab_ref, tiles, comb_scr):
    row_iota = lax.broadcasted_iota(jnp.int32, (TOK_GROUP, D_MODEL), 0)
    lane = lax.broadcasted_iota(jnp.int32, (TOK_GROUP, 2 * N_SEL), 1)

    def group(base):
        xs = h2_ref[pl.ds(base, TOK_GROUP), :].astype(F32)
        acc = None
        for i in range(TOK_GROUP):
            tile = tiles[i % 2]
            _gather_rows(idx_ref, base + i, tab_ref, tile)
            w = jnp.concatenate([_tile_chunk(tile, j) for j in range(ROW_WORDS)], axis=-1)
            xm = jnp.where(row_iota == i, xs, 0.0)
            lhs = jnp.concatenate([xm[:, :D_MODEL // 2], xm[:, D_MODEL // 2:]], axis=0)
            d = _dot_nt(lhs.astype(BF16), w)
            acc = d if acc is None else acc + d
            yield
        comb_scr[pl.ds(base, TOK_GROUP), :] = jnp.where(lane % 2 == 0, acc[:TOK_GROUP], acc[TOK_GROUP:])

    return group


def _u_finish(comb_scr, gate_ref, ssum_ref, pa_ref, pb_ref, w_ref):
    a = _dot_split(comb_scr[...], ssum_ref[...])
    act = 0.5 * a * (1.0 + lax.erf(a * (2.0 ** -0.5)))
    w = (gate_ref[...] * act).astype(BF16)
    w_ref[:, :2 * N_SEL] = _dot(w, pa_ref[...]).astype(BF16)
    w_ref[:, 2 * N_SEL:] = _dot(w, pb_ref[...]).astype(BF16)


def _peer_u_kernel(idx_ref, h2_ref, gate_ref, tab_ref, ssum_ref, pa_ref, pb_ref, w_ref,
                   tile0_scr, tile1_scr, comb_scr):
    group = _u_group_fn(idx_ref, h2_ref, tab_ref, (tile0_scr, tile1_scr), comb_scr)

    def body(g, carry):
        for j in range(GROUPS_PER_TRIP):
            _drain(group(pl.multiple_of((g * GROUPS_PER_TRIP + j) * TOK_GROUP, TOK_GROUP)))
        return carry

    lax.fori_loop(0, h2_ref.shape[0] // (GROUPS_PER_TRIP * TOK_GROUP), body, 0)
    _u_finish(comb_scr, gate_ref, ssum_ref, pa_ref, pb_ref, w_ref)


def _peer_u_select_kernel(idx_ref, h2_ref, gate_ref, tab_ref, ssum_ref, pa_ref, pb_ref,
                          h2n_ref, wqt_ref, sk_ref, w_ref, eidx_ref, gaten_ref,
                          tile0_scr, tile1_scr, comb_scr, qt_scr):
    tm = h2_ref.shape[0]
    per_head = tm // PEER_HEADS
    qt_scr[...] = _dot_nt(wqt_ref[...], h2n_ref[...]).astype(BF16)
    group = _u_group_fn(idx_ref, h2_ref, tab_ref, (tile0_scr, tile1_scr), comb_scr)
    head = _select_head_fn(qt_scr, sk_ref, eidx_ref, gaten_ref, tm)

    heads_per_trip = 2

    def gather_steps(t):
        for g in range(heads_per_trip * per_head // TOK_GROUP):
            yield from group(pl.multiple_of(t * heads_per_trip * per_head + g * TOK_GROUP, TOK_GROUP))

    def select_steps(t):
        for j in range(heads_per_trip):
            yield from head(t * heads_per_trip + j)

    def body(t, carry):
        _alternate((gather_steps(t), 1), (select_steps(t), 3))
        return carry

    lax.fori_loop(0, PEER_HEADS // heads_per_trip, body, 0)
    _u_finish(comb_scr, gate_ref, ssum_ref, pa_ref, pb_ref, w_ref)


def _peer_v_kernel(idx_ref, w_ref, x1_ref, mod_ref, fg_ref, tab_ref, o_ref, tile0_scr, tile1_scr, *, final):
    tm = w_ref.shape[0]
    row_iota = lax.broadcasted_iota(jnp.int32, (TOK_GROUP, 4 * N_SEL), 0)
    g2 = mod_ref[5:6, :]
    tiles = (tile0_scr, tile1_scr)

    def group(base):
        ws = w_ref[pl.ds(base, TOK_GROUP), :].astype(F32)
        accs = None
        for i in range(TOK_GROUP):
            tile = tiles[i % 2]
            _gather_rows(idx_ref, base + i, tab_ref, tile)
            wm = jnp.where(row_iota == i, ws, 0.0)
            lhs = jnp.concatenate([wm[:, :2 * N_SEL], wm[:, 2 * N_SEL:]], axis=0).astype(BF16)
            ds = [_dot(lhs, _tile_chunk(tile, j)) for j in range(ROW_WORDS)]
            accs = ds if accs is None else [a + d for a, d in zip(accs, ds)]
        y = jnp.concatenate([a[:TOK_GROUP] for a in accs] + [a[TOK_GROUP:] for a in accs], axis=-1)
        x2 = x1_ref[pl.ds(base, TOK_GROUP), :] + g2 * y
        if final:
            x2 = _rms(x2) * fg_ref[...]
        o_ref[pl.ds(base, TOK_GROUP), :] = x2

    def body(g, carry):
        for j in range(GROUPS_PER_TRIP):
            group(pl.multiple_of((g * GROUPS_PER_TRIP + j) * TOK_GROUP, TOK_GROUP))
        return carry

    lax.fori_loop(0, tm // (GROUPS_PER_TRIP * TOK_GROUP), body, 0)


def _table_spec(tab):
    return pl.BlockSpec(tab.shape, lambda i: (0, 0), pipeline_mode=pl.Buffered(1))


def _u_specs(tm, tile_off, utab):
    in_specs = [pl.BlockSpec((tm, N_SEL), lambda i: (i, 0), memory_space=pltpu.SMEM),
                pl.BlockSpec((tm, D_MODEL), lambda i: (i + tile_off, 0)),
                pl.BlockSpec((tm, N_SEL), lambda i: (i, 0)),
                _table_spec(utab),
                pl.BlockSpec((2 * N_SEL, N_SEL), lambda i: (0, 0)),
                pl.BlockSpec((N_SEL, 2 * N_SEL), lambda i: (0, 0)),
                pl.BlockSpec((N_SEL, 2 * N_SEL), lambda i: (0, 0))]
    scratch = [pltpu.VMEM((ROW_WORDS * GATHER_STRIDE, 128), jnp.uint32),
               pltpu.VMEM((ROW_WORDS * GATHER_STRIDE, 128), jnp.uint32),
               pltpu.VMEM((tm, 2 * N_SEL), F32)]
    return in_specs, pl.BlockSpec((tm, 4 * N_SEL), lambda i: (i, 0)), scratch


def _peer_u(eidx, h2, gate, utab, tabs, tm, tile_off):
    n_tok = eidx.shape[0]
    in_specs, out_spec, scratch = _u_specs(tm, tile_off, utab)
    return pl.pallas_call(
        _peer_u_kernel,
        grid=(n_tok // tm,),
        in_specs=in_specs,
        out_specs=out_spec,
        out_shape=jax.ShapeDtypeStruct((n_tok, 4 * N_SEL), BF16),
        scratch_shapes=scratch,
        compiler_params=_params(("arbitrary",)),
        name="peer_u",
    )(eidx, h2, gate, utab, tabs["ssum"], tabs["pa"], tabs["pb"])


def _peer_u_select(eidx, h2, gate, utab, tabs, lw, tm, tile_off):
    n_tok = eidx.shape[0]
    in_specs, out_spec, scratch = _u_specs(tm, tile_off, utab)
    sel_in, sel_out, sel_shape, qt = _select_specs(tm, tile_off + n_tok // tm)
    return pl.pallas_call(
        _peer_u_select_kernel,
        grid=(n_tok // tm,),
        in_specs=in_specs + sel_in,
        out_specs=[out_spec] + sel_out,
        out_shape=[jax.ShapeDtypeStruct((n_tok, 4 * N_SEL), BF16)] + sel_shape(n_tok),
        scratch_shapes=scratch + [qt],
        compiler_params=_params(("arbitrary",)),
        name="peer_u_select",
    )(eidx, h2, gate, utab, tabs["ssum"], tabs["pa"], tabs["pb"], h2, lw["wqt"], lw["sk"])


def _peer_v(eidx, w, x1, mod_l, b_idx, final_g, vtab, tm, final):
    T = w.shape[0]
    tiles_per_batch = T // tm // b_idx[1]
    return pl.pallas_call(
        functools.partial(_peer_v_kernel, final=final),
        grid=(T // tm,),
        in_specs=[pl.BlockSpec((tm, N_SEL), lambda i: (i, 0), memory_space=pltpu.SMEM),
                  pl.BlockSpec((tm, 4 * N_SEL), lambda i: (i, 0)),
                  pl.BlockSpec((tm, D_MODEL), lambda i: (i, 0)),
                  pl.BlockSpec((None, 6, D_MODEL),
                               lambda i: (i // tiles_per_batch + b_idx[0], 0, 0)),
                  pl.BlockSpec((1, D_MODEL), lambda i: (0, 0)),
                  _table_spec(vtab)],
        out_specs=pl.BlockSpec((tm, D_MODEL), lambda i: (i, 0)),
        out_shape=jax.ShapeDtypeStruct((T, D_MODEL), F32),
        scratch_shapes=[pltpu.VMEM((ROW_WORDS * GATHER_STRIDE, 128), jnp.uint32),
                        pltpu.VMEM((ROW_WORDS * GATHER_STRIDE, 128), jnp.uint32)],
        compiler_params=_params(("arbitrary",)),
        name="peer_v",
    )(eidx, w, x1, mod_l, final_g, vtab)


def _pack_table(t):
    n = t.shape[0]
    b = lax.bitcast_convert_type(t.astype(BF16), jnp.uint16).astype(jnp.uint32)
    words = b[:, :D_MODEL // 2] | (b[:, D_MODEL // 2:] << 16)
    return words.reshape(n * ROW_WORDS, 128)


def _rope_lane_tables(S, dim):
    dq = dim // 4
    freqs = ROPE_THETA ** (-jnp.arange(dq, dtype=F32) / dq)
    t = jnp.arange(S, dtype=jnp.int32)
    row = (t // GRID_W).astype(F32)
    col = (t % GRID_W).astype(F32)
    ang = jnp.stack([row[:, None] * freqs, col[:, None] * freqs], axis=1)
    cos, sin = jnp.cos(ang), jnp.sin(ang)
    d = np.arange(dim)
    blk, f = d // (2 * dq), d % dq
    sign = np.where((d % (2 * dq)) < dq, -1.0, 1.0).astype(np.float32)
    return cos[:, blk, f], sin[:, blk, f] * sign


def _const_tables(S):
    cos_a, sin_a = _rope_lane_tables(S, HEAD_DIM)
    cos_c, sin_c = _rope_lane_tables(S, C_ROPE)
    one = jnp.ones((S, 128), F32)
    zero = jnp.zeros((S, 128), F32)
    cosk = one.at[:, :C_ROPE].set(cos_c)
    sink = zero.at[:, :C_ROPE].set(sin_c)
    cosh = one.at[:, C_NOPE:C_NOPE + C_ROPE].set(cos_c)
    sinh = zero.at[:, C_NOPE:C_NOPE + C_ROPE].set(sin_c)
    head_of = np.arange(A_Q) // HEAD_DIM
    epl = np.zeros((128, C_QW), np.float32)
    for h in range(C_HEADS):
        epl[np.arange(C_ROPE), h * 128 + C_NOPE + np.arange(C_ROPE)] = 1.0
    k = np.arange(N_SEL)
    ssum = np.zeros((2 * N_SEL, N_SEL), np.float32)
    ssum[2 * k, k] = 1.0
    ssum[2 * k + 1, k] = 1.0
    pa = np.zeros((N_SEL, 2 * N_SEL), np.float32)
    pb = np.zeros((N_SEL, 2 * N_SEL), np.float32)
    pa[k, 2 * k] = 1.0
    pb[k, 2 * k + 1] = 1.0
    return {
        "cosa": jnp.tile(cos_a, (1, A_HEADS)), "sina": jnp.tile(sin_a, (1, A_HEADS)),
        "cosc": jnp.tile(cosh, (1, C_HEADS)), "sinc": jnp.tile(sinh, (1, C_HEADS)),
        "cosk": cosk, "sink": sink,
        "m64": jnp.asarray(head_of[:, None] == head_of[None, :], BF16),
        "epl": jnp.asarray(epl, BF16), "ssum": jnp.asarray(ssum, BF16),
        "pa": jnp.asarray(pa, BF16), "pb": jnp.asarray(pb, BF16),
    }


def _layer_weights(l, norm1_g, norm2_g, w_in, a_q_norm, a_k_norm, c_q_norm, c_wqb, c_kv_norm, c_wkvb,
                   w_out, peer_wq, peer_subkeys, peer_u, peer_v):
    src = np.cumsum((0, A_Q, A_KV, A_KV, B_W, B_W, B_W, C_Q_RANK, C_KV_RANK))
    dst = (OFF_AQ, OFF_AK, OFF_AV, OFF_BQ, OFF_BK, OFF_BV, OFF_CQA, OFF_CKV, OFF_KPE)
    wid = (A_Q, A_KV, A_KV, B_W, B_W, B_W, C_Q_RANK, C_KV_RANK, C_ROPE)
    w1 = jnp.zeros((D_MODEL, IN_W), F32)
    for s, d, w in zip(src, dst, wid):
        w1 = w1.at[:, d:d + w].set(w_in[l][:, s:s + w])
    wqb = jnp.zeros((C_Q_RANK, C_HEADS, 128), F32).at[:, :, :C_NOPE + C_ROPE].set(
        c_wqb[l].reshape(C_Q_RANK, C_HEADS, C_NOPE + C_ROPE)).reshape(C_Q_RANK, C_QW)
    wkvb = c_wkvb[l].reshape(C_KV_RANK, C_HEADS, C_NOPE + C_V)
    wkn = jnp.zeros((C_KV_RANK, C_HEADS, 128), F32).at[:, :, :C_NOPE].set(
        wkvb[:, :, :C_NOPE]).reshape(C_KV_RANK, C_QW)
    wkv = wkvb[:, :, C_NOPE:].reshape(C_KV_RANK, C_VW)
    return {
        "g1": norm1_g[l][None, :], "g2": norm2_g[l][None, :],
        "w1": w1.astype(BF16),
        "aqn": jnp.tile(a_q_norm[l], A_HEADS)[None, :], "akn": jnp.tile(a_k_norm[l], A_KV_HEADS)[None, :],
        "cqn": c_q_norm[l][None, :], "ckvn": c_kv_norm[l][None, :],
        "wqb": wqb.astype(BF16), "wkn": wkn.astype(BF16), "wkv": wkv.astype(BF16),
        "wout": w_out[l].astype(BF16),
        "wqt": peer_wq[l].T.astype(BF16),
        "sk": peer_subkeys[l].reshape(2 * PEER_HEADS, N_KEYS, PEER_DKEY // 2).astype(BF16),
        "utab": _pack_table(peer_u[l]), "vtab": _pack_table(peer_v[l]),
    }


def _tile(n, pref):
    t = pref
    while n % t:
        t //= 2
    return t


def _trunk(x, mod, b_off, n_batch_total, layers, na_bias, final_g):
    B, S, _ = x.shape
    T = B * S
    tabs = _const_tables(S)
    tm = _tile(S, 512)
    tq = _tile(S, 256)
    tp = _tile(S, 256)
    depth = len(layers)
    for l, lw in enumerate(layers):
        aq, ak, av, bq, bk, bv, cq, ck, cv = _in_proj(x, mod[l], b_off, lw, tabs, tm)
        oa = _dense_attention(_attn_a_kernel, aq, ak, av, A_Q, tq, "attn_a")
        ob = _na_attention(bq, bk, bv, na_bias[l])
        oc = _dense_attention(_attn_c_kernel, cq, ck, cv, C_VW, tq, "attn_c")
        x1, h2 = _out_proj(oa, ob, oc, x, mod[l], b_off, lw, tm)
        h2 = h2.reshape(T, D_MODEL)
        n_chunk = PEER_CHUNKS
        chunk = T // n_chunk
        tiles = chunk // tp
        sel = _peer_select(h2, lw, tp, 0, chunk)
        eidx, w = [], []
        for c in range(n_chunk):
            e_c, g_c = sel[0].T, sel[1].T
            if c + 1 < n_chunk:
                w_c, *sel = _peer_u_select(e_c, h2, g_c, lw["utab"], tabs, lw, tp, c * tiles)
            else:
                w_c = _peer_u(e_c, h2, g_c, lw["utab"], tabs, tp, c * tiles)
            eidx.append(e_c)
            w.append(w_c)
        eidx, w = jnp.concatenate(eidx, axis=0), jnp.concatenate(w, axis=0)
        x = _peer_v(eidx, w, x1.reshape(T, D_MODEL), mod[l], (b_off, B), final_g[None, :],
                    lw["vtab"], tp, l == depth - 1).reshape(B, S, D_MODEL)
    return x


def kernel(x_prompt, x_sample, c_prompt, c_sample, ada_w, ada_b, norm1_g, norm2_g, w_in, a_q_norm, a_k_norm, b_rpb, c_q_norm, c_wqb, c_kv_norm, c_wkvb, w_out, peer_wq, peer_subkeys, peer_u, peer_v, final_g):
    depth = ada_w.shape[0]
    c_all = jnp.concatenate([c_prompt, c_sample], axis=0)
    mod = _modulation(c_all, ada_w, ada_b).reshape(depth, c_all.shape[0], 6, D_MODEL)
    layers = [_layer_weights(l, norm1_g, norm2_g, w_in, a_q_norm, a_k_norm, c_q_norm, c_wqb, c_kv_norm,
                             c_wkvb, w_out, peer_wq, peer_subkeys, peer_u, peer_v) for l in range(depth)]
    outs = []
    b_off = 0
    for x in (x_prompt, x_sample):
        rows = x.shape[1] // GRID_W
        na_bias = [_na_bias(b_rpb[l], rows) for l in range(depth)]
        outs.append(_trunk(x, mod, b_off, c_all.shape[0], layers, na_bias, final_g))
        b_off += x.shape[0]
    return tuple(outs)
```
